```python
import math
import jax, jax.numpy as jnp
from jax import lax
import numpy as np

D_MODEL = 1024
BATCH = 16
SEQ = 2048
DEPTH = 2
DEC_BATCH = 32
DEC_SEQ = 8
PAST_LEN = 16384
PAGE_SIZE = 128

N_META = 16
A_HEADS = 8
A_DK = 128
A_DV = D_MODEL // A_HEADS
A_CHUNK = 16
B_HEADS = 8
B_DQK = D_MODEL // B_HEADS
B_DV = D_MODEL // B_HEADS
B_CHUNK = 128
C_HEADS = 8
C_DH = D_MODEL // C_HEADS
Q_BLOCK = 128
SB_BIAS_INIT = -7.0
D_FF = 2816
EPS = 1e-6
A_QW = A_HEADS * A_DK
B_QW = B_HEADS * B_DQK
IN_SPLITS = (A_QW, A_QW, D_MODEL, D_MODEL,
             B_QW, B_QW, D_MODEL, D_MODEL, B_HEADS, B_HEADS,
             D_MODEL, D_MODEL, D_MODEL,
             D_MODEL, D_MODEL, D_MODEL)
IN_WIDTH = sum(IN_SPLITS)

kernel_name = 'hybrid_hgrn2_mlstm_stickbreak_step'


def rms_norm(x, g):
    x32 = x.astype(jnp.float32)
    y = x32 * lax.rsqrt(jnp.mean(x32 * x32, axis=-1, keepdims=True) + EPS)
    return (y * g.astype(jnp.float32)).astype(x.dtype)


def head_norm(x, g):
    y = x * lax.rsqrt(jnp.mean(x * x, axis=-1, keepdims=True) + EPS)
    return y.reshape(x.shape[0], x.shape[1], -1) * g.astype(jnp.float32)


def swiglu(x, w_up, w_down):
    gate, up = jnp.split(x @ w_up, 2, axis=-1)
    return (jax.nn.silu(gate) * up) @ w_down


def scan_chunks(step, state, seqs, chunk):
    bsz, length = seqs[0].shape[:2]
    n = length // chunk
    xs = tuple(jnp.swapaxes(s.reshape(bsz, n, chunk, *s.shape[2:]), 0, 1) for s in seqs)
    state, ys = lax.scan(lambda st, x: step(st, *x), state, xs)
    ys = jnp.swapaxes(ys, 0, 1)
    return state, ys.reshape(bsz, length, *ys.shape[3:])


def run_chunked(step, state, seqs, chunk, lead):
    outs = []
    if lead > 0:
        state, o = scan_chunks(step, state, tuple(s[:, :lead] for s in seqs), lead)
        outs.append(o)
        seqs = tuple(s[:, lead:] for s in seqs)
    state, o = scan_chunks(step, state, seqs, math.gcd(seqs[0].shape[1], chunk))
    outs.append(o)
    return state, jnp.concatenate(outs, axis=1)


def hgrn2_chunk(S, q, logf, k, i):
    c = q.shape[1]
    cum = jnp.cumsum(logf, axis=1)
    inter = jnp.einsum('bthk,bhkv->bthv', q * jnp.exp(cum), S)
    causal = jnp.tril(jnp.ones((c, c), bool))[None, :, :, None, None]
    decay = jnp.exp(jnp.where(causal, cum[:, :, None] - cum[:, None, :], -jnp.inf))
    scores = jnp.einsum('bthk,bshk,btshk->bhts', q, k, decay)
    o = inter + jnp.einsum('bhts,bshv->bthv', scores, i)
    last = cum[:, -1]
    S_new = jnp.exp(last)[..., None] * S + jnp.einsum('bshk,bshv->bhkv', k * jnp.exp(last[:, None] - cum), i)
    return S_new, o


def mlstm_chunk(state, q, k, v, ig, logf):
    C, n, m = state
    c = q.shape[1]
    b = jnp.swapaxes(jnp.cumsum(logf, axis=1), 1, 2)
    igT = jnp.swapaxes(ig, 1, 2)
    causal = jnp.tril(jnp.ones((c, c), bool))
    logD = jnp.where(causal, b[..., :, None] - b[..., None, :] + igT[..., None, :], -jnp.inf)
    m_t = jnp.maximum(b + m[..., None], jnp.max(logD, axis=-1))
    w = jnp.exp(logD - m_t[..., None])
    inter = jnp.exp(b + m[..., None] - m_t)
    ws = w * jnp.einsum('bthd,bshd->bhts', q, k)
    num = inter[..., None] * jnp.einsum('bthk,bhkv->bhtv', q, C) + jnp.einsum('bhts,bshv->bhtv', ws, v)
    den = inter * jnp.einsum('bthk,bhk->bht', q, n) + ws.sum(-1)
    h = num / jnp.maximum(jnp.abs(den), jnp.exp(-m_t))[..., None]
    m_new = m_t[..., -1]
    wk = jnp.exp(b[..., -1:] - b + igT - m_new[..., None])
    carry = jnp.exp(b[..., -1] + m - m_new)
    C_new = carry[..., None, None] * C + jnp.einsum('bhs,bshk,bshv->bhkv', wk, k, v)
    n_new = carry[..., None] * n + jnp.einsum('bhs,bshk->bhk', wk, k)
    return (C_new, n_new, m_new), jnp.swapaxes(h, 1, 2)


def sb_segment(z, v, mask, carry):
    log_keep = jnp.where(mask, jax.nn.log_sigmoid(-z), 0.0)
    after = jnp.flip(jnp.cumsum(jnp.flip(log_keep, -1), axis=-1), -1) - log_keep
    w = jnp.where(mask, jnp.exp(jax.nn.log_sigmoid(z) + after + carry[..., None]), 0.0)
    return jnp.einsum('bhqk,bkhd->bqhd', w, v), carry + log_keep.sum(-1)


def sb_prompt(q, k, v, bias, lead):
    bsz, length, n_heads, dh = q.shape
    bias4 = bias[None, :, None, None]

    def block(qb, qpos, keys, vals):
        z = jnp.einsum('bqhd,bkhd->bhqk', qb, keys) + bias4
        mask = jnp.arange(keys.shape[1])[None, :] < qpos[:, None]
        out, _ = sb_segment(z, vals, mask, jnp.zeros(z.shape[:3], z.dtype))
        return out

    outs = []
    if lead > 0:
        outs.append(block(q[:, :lead], jnp.arange(lead), k[:, :lead], v[:, :lead]))
    rest = length - lead
    qb_len = math.gcd(rest, Q_BLOCK)
    n_blk = rest // qb_len
    q_blocks = jnp.swapaxes(q[:, lead:].reshape(bsz, n_blk, qb_len, n_heads, dh), 0, 1)
    starts = lead + qb_len * jnp.arange(n_blk)
    o = lax.map(lambda a: block(a[0], a[1] + jnp.arange(qb_len), k, v), (q_blocks, starts))
    outs.append(jnp.swapaxes(o, 0, 1).reshape(bsz, rest, n_heads, dh))
    return jnp.concatenate(outs, axis=1)


def sb_sample(q, k, v, bias, cache_k, cache_v, page_table, layer):
    n_new = q.shape[1]
    page = cache_k.shape[2]
    bias4 = bias[None, :, None, None]
    z = jnp.einsum('bqhd,bkhd->bhqk', q, k) + bias4
    out, carry = sb_segment(z, v, jnp.tril(jnp.ones((n_new, n_new), bool), -1), jnp.zeros(z.shape[:3], z.dtype))
    full = jnp.ones((n_new, page), bool)

    def page_step(acc, pages):
        o_acc, c_acc = acc
        kp = cache_k[pages, layer].astype(jnp.float32)
        vp = cache_v[pages, layer].astype(jnp.float32)
        zp = jnp.einsum('bqhd,bkhd->bhqk', q, kp) + bias4
        o, c_acc = sb_segment(zp, vp, full, c_acc)
        return (o_acc + o, c_acc), None

    (out, _), _ = lax.scan(page_step, (out, carry), jnp.flip(page_table, axis=1).T)
    return out


def trunk(h, lead, st_hgrn, st_C, st_n, st_m, attend, p):
    bsz, length, _ = h.shape
    f32 = jnp.float32
    offsets = np.cumsum(IN_SPLITS)[:-1].tolist()
    lb_all = jnp.cumsum(jax.nn.softmax(p['lb_param'].astype(f32), axis=0), axis=0)
    lb_all = lb_all - lb_all[0]

    def heads(t, n):
        return t.reshape(bsz, length, n, -1)

    out_S, out_C, out_n, out_m, out_k, out_v = [], [], [], [], [], []
    for l in range(DEPTH):
        h = h + 0.5 * swiglu(rms_norm(h, p['g_ffn1'][l]), p['w_ffn1_up'][l], p['w_ffn1_down'][l])
        u = rms_norm(h, p['g_mix'][l])
        (a_q, a_f, a_i, a_g, b_q, b_k, b_v, b_o, b_ig, b_fg,
         c_q, c_k, c_v, g_a, g_b, g_c) = jnp.split((u @ p['w_in'][l]).astype(f32), offsets, axis=-1)
        lb = lb_all[l]
        a_logf = jnp.logaddexp(jnp.log(lb), jnp.log1p(-lb) + jax.nn.log_sigmoid(a_f))
        a_k = (1.0 - lb) * jax.nn.sigmoid(-a_f)
        S_l, o_a = run_chunked(hgrn2_chunk, st_hgrn[l].astype(f32),
                               (heads(a_q, A_HEADS), heads(a_logf, A_HEADS), heads(a_k, A_HEADS), heads(a_i, A_HEADS)),
                               A_CHUNK, lead)
        h_a = head_norm(o_a, p['g_norm_a'][l]) * jax.nn.silu(a_g)
        b_state = (st_C[l].astype(f32), st_n[l].astype(f32), st_m[l].astype(f32))
        (C_l, n_l, m_l), o_b = run_chunked(
            mlstm_chunk, b_state,
            (heads(b_q, B_HEADS), heads(b_k, B_HEADS) * B_DQK ** -0.5, heads(b_v, B_HEADS),
             b_ig + p['b_ig'][l].astype(f32), jax.nn.log_sigmoid(b_fg + p['b_fg'][l].astype(f32))),
            B_CHUNK, lead)
        h_b = head_norm(o_b, p['g_norm_b'][l]) * jax.nn.sigmoid(b_o)
        k_c = heads(c_k, C_HEADS)
        v_c = heads(c_v, C_HEADS)
        h_c = attend(l, heads(c_q, C_HEADS) * C_DH ** -0.5, k_c, v_c,
                     p['b_sb'][l].astype(f32)).reshape(bsz, length, D_MODEL)
        merged = jax.nn.sigmoid(g_a) * h_a + jax.nn.sigmoid(g_b) * h_b + jax.nn.sigmoid(g_c) * h_c
        h = h + merged.astype(h.dtype) @ p['w_out'][l]
        h = h + 0.5 * swiglu(rms_norm(h, p['g_ffn2'][l]), p['w_ffn2_up'][l], p['w_ffn2_down'][l])
        out_S.append(S_l)
        out_C.append(C_l)
        out_n.append(n_l)
        out_m.append(m_l)
        out_k.append(k_c)
        out_v.append(v_c)
    y = rms_norm(h, p['g_final'])
    return (y, jnp.stack(out_S), jnp.stack(out_C), jnp.stack(out_n), jnp.stack(out_m),
            jnp.stack(out_k, axis=1).astype(h.dtype), jnp.stack(out_v, axis=1).astype(h.dtype))


def setup_inputs(seed: int = 0) -> dict:
    key = jax.random.key(seed)
    ks = jax.random.split(key, 32)
    n_pages = PAST_LEN // PAGE_SIZE
    in_use = DEC_BATCH * n_pages
    n_pool = in_use + max(1, in_use // 4)
    nrm = jax.random.normal
    f32 = jnp.float32
    page_table = jax.random.permutation(ks[0], n_pool)[:in_use].reshape(DEC_BATCH, n_pages).astype(jnp.int32)
    return {
        'x_prompt': nrm(ks[1], (BATCH, SEQ, D_MODEL), f32),
        'x_sample': nrm(ks[2], (DEC_BATCH, DEC_SEQ, D_MODEL), f32),
        'state_hgrn': nrm(ks[3], (DEPTH, DEC_BATCH, A_HEADS, A_DK, A_DV), f32),
        'state_mlstm_C': nrm(ks[4], (DEPTH, DEC_BATCH, B_HEADS, B_DQK, B_DV), f32),
        'state_mlstm_n': nrm(ks[5], (DEPTH, DEC_BATCH, B_HEADS, B_DQK), f32),
        'state_mlstm_m': 0.5 * nrm(ks[6], (DEPTH, DEC_BATCH, B_HEADS), f32),
        'cache_k': nrm(ks[7], (n_pool, DEPTH, PAGE_SIZE, C_HEADS, C_DH), f32),
        'cache_v': nrm(ks[8], (n_pool, DEPTH, PAGE_SIZE, C_HEADS, C_DH), f32),
        'page_table': page_table,
        'meta_tokens': nrm(ks[9], (N_META, D_MODEL), f32),
        'g_ffn1': 1.0 + 0.01 * nrm(ks[10], (DEPTH, D_MODEL), f32),
        'w_ffn1_up': nrm(ks[11], (DEPTH, D_MODEL, 2 * D_FF), f32) * D_MODEL ** -0.5,
        'w_ffn1_down': nrm(ks[12], (DEPTH, D_FF, D_MODEL), f32) * D_FF ** -0.5,
        'g_mix': 1.0 + 0.01 * nrm(ks[13], (DEPTH, D_MODEL), f32),
        'w_in': nrm(ks[14], (DEPTH, D_MODEL, IN_WIDTH), f32) * D_MODEL ** -0.5,
        'lb_param': nrm(ks[15], (DEPTH, A_QW), f32),
        'b_ig': 0.1 * nrm(ks[16], (DEPTH, B_HEADS), f32),
        'b_fg': jnp.linspace(3.0, 6.0, B_HEADS, dtype=f32)[None] + 0.1 * nrm(ks[17], (DEPTH, B_HEADS), f32),
        'b_sb': SB_BIAS_INIT + 0.1 * nrm(ks[25], (DEPTH, C_HEADS), f32),
        'g_norm_a': 1.0 + 0.01 * nrm(ks[18], (DEPTH, A_HEADS * A_DV), f32),
        'g_norm_b': 1.0 + 0.01 * nrm(ks[19], (DEPTH, B_HEADS * B_DV), f32),
        'w_out': nrm(ks[20], (DEPTH, D_MODEL, D_MODEL), f32) * D_MODEL ** -0.5,
        'g_ffn2': 1.0 + 0.01 * nrm(ks[21], (DEPTH, D_MODEL), f32),
        'w_ffn2_up': nrm(ks[22], (DEPTH, D_MODEL, 2 * D_FF), f32) * D_MODEL ** -0.5,
        'w_ffn2_down': nrm(ks[23], (DEPTH, D_FF, D_MODEL), f32) * D_FF ** -0.5,
        'g_final': 1.0 + 0.01 * nrm(ks[24], (D_MODEL,), f32),
    }


def reference(x_prompt, x_sample, state_hgrn, state_mlstm_C, state_mlstm_n, state_mlstm_m,
              cache_k, cache_v, page_table, meta_tokens, g_ffn1, w_ffn1_up, w_ffn1_down,
              g_mix, w_in, lb_param, b_ig, b_fg, b_sb, g_norm_a, g_norm_b, w_out,
              g_ffn2, w_ffn2_up, w_ffn2_down, g_final):
    p = dict(g_ffn1=g_ffn1, w_ffn1_up=w_ffn1_up, w_ffn1_down=w_ffn1_down, g_mix=g_mix, w_in=w_in,
             lb_param=lb_param, b_ig=b_ig, b_fg=b_fg, b_sb=b_sb, g_norm_a=g_norm_a, g_norm_b=g_norm_b,
             w_out=w_out, g_ffn2=g_ffn2, w_ffn2_up=w_ffn2_up, w_ffn2_down=w_ffn2_down, g_final=g_final)
    f32 = jnp.float32
    bsz = x_prompt.shape[0]
    meta = jnp.broadcast_to(meta_tokens.astype(x_prompt.dtype)[None], (bsz, N_META, D_MODEL))
    h0 = jnp.concatenate([meta, x_prompt], axis=1)
    zS = jnp.zeros((DEPTH, bsz, A_HEADS, A_DK, A_DV), f32)
    zC = jnp.zeros((DEPTH, bsz, B_HEADS, B_DQK, B_DV), f32)
    zn = jnp.zeros((DEPTH, bsz, B_HEADS, B_DQK), f32)
    zm = jnp.zeros((DEPTH, bsz, B_HEADS), f32)
    y_full, p_S, p_C, p_n, p_m, p_k, p_v = trunk(
        h0, N_META, zS, zC, zn, zm, lambda l, q, k, v, bias: sb_prompt(q, k, v, bias, N_META), p)
    y_prompt = y_full[:, N_META:]
    y_sample, s_S, s_C, s_n, s_m, s_k, s_v = trunk(
        x_sample, 0, state_hgrn, state_mlstm_C, state_mlstm_n, state_mlstm_m,
        lambda l, q, k, v, bias: sb_sample(q, k, v, bias, cache_k, cache_v, page_table, l), p)
    return (y_prompt, y_sample, p_S, p_C, p_n, p_m, p_k, p_v, s_S, s_C, s_n, s_m, s_k, s_v)
```

```python
import functools

import jax
import jax.numpy as jnp
from jax import lax
from jax.experimental import pallas as pl
from jax.experimental.pallas import tpu as pltpu

F32 = jnp.float32
BF16 = jnp.bfloat16

D_MODEL = 1024
N_HEADS = 8
D_HEAD = 128
D_FF = 2816
N_META = 16
EPS = 1e-6

LANES = 128
VMEM_LIMIT_BYTES = 56 * 1024 * 1024

COL_A_Q, COL_A_F, COL_A_I, COL_A_G = 0, 8, 16, 24
COL_B_Q, COL_B_K, COL_B_V, COL_B_O = 32, 40, 48, 56
COL_C_Q, COL_C_K, COL_C_V = 64, 72, 80
COL_G_A, COL_G_B, COL_G_C = 88, 96, 104
MAIN_WIDTH = 112 * LANES
GATE_OFFSET = 8 * D_MODEL

HGRN_CHUNK = 128
HGRN_SUB = 16
MLSTM_CHUNK = 128
SB_BLOCK = 128

FF_TILE = 256
IN_TILE = 1024
MERGE_TILE = 256
SB_SAMPLE_ROWS = 16


def _cparams(*sem):
    return pltpu.CompilerParams(dimension_semantics=sem, vmem_limit_bytes=VMEM_LIMIT_BYTES)


def _dot(a, b):
    return jnp.dot(a, b, preferred_element_type=F32)


def _dot_nt(a, b):
    return lax.dot_general(a, b, (((1,), (1,)), ((), ())), preferred_element_type=F32)


def _dot_tn(a, b):
    return lax.dot_general(a, b, (((0,), (0,)), ((), ())), preferred_element_type=F32)


def _rms(x, g):
    return x * lax.rsqrt(jnp.mean(x * x, axis=-1, keepdims=True) + EPS) * g


def _softplus_neg_abs(z):
    return jnp.log1p(jnp.exp(-jnp.abs(z)))


def _log_sigmoid(z):
    return jnp.minimum(z, 0.0) - _softplus_neg_abs(z)


def _split_bf16(x):
    hi = x.astype(BF16)
    lo = (x - hi.astype(F32)).astype(BF16)
    return hi, lo


def _iota2(shape, dim):
    return lax.broadcasted_iota(jnp.int32, shape, dim)


def _ffn_kernel(h_ref, g_ref, wg_ref, wu_ref, wd_ref, o_ref, xn_ref, acc_ref):
    f = pl.program_id(1)

    @pl.when(f == 0)
    def _():
        xn_ref[...] = _rms(h_ref[...], g_ref[...]).astype(BF16)
        acc_ref[...] = jnp.zeros_like(acc_ref)

    xn = xn_ref[...]
    gate = _dot(xn, wg_ref[...])
    up = _dot(xn, wu_ref[...])
    act = (gate * jax.nn.sigmoid(gate) * up).astype(BF16)
    acc_ref[...] += _dot(act, wd_ref[...])

    @pl.when(f == pl.num_programs(1) - 1)
    def _():
        o_ref[...] = h_ref[...] + 0.5 * acc_ref[...]


def _ffn(h, g, w_up, w_down, tm):
    t = h.shape[0]
    nf = D_FF // FF_TILE
    return pl.pallas_call(
        _ffn_kernel,
        out_shape=jax.ShapeDtypeStruct((t, D_MODEL), F32),
        grid=(t // tm, nf),
        in_specs=[
            pl.BlockSpec((tm, D_MODEL), lambda i, f: (i, 0)),
            pl.BlockSpec((1, D_MODEL), lambda i, f: (0, 0)),
            pl.BlockSpec((D_MODEL, FF_TILE), lambda i, f: (0, f)),
            pl.BlockSpec((D_MODEL, FF_TILE), lambda i, f: (0, f + D_FF // FF_TILE)),
            pl.BlockSpec((FF_TILE, D_MODEL), lambda i, f: (f, 0)),
        ],
        out_specs=pl.BlockSpec((tm, D_MODEL), lambda i, f: (i, 0)),
        scratch_shapes=[pltpu.VMEM((tm, D_MODEL), BF16), pltpu.VMEM((tm, D_MODEL), F32)],
        compiler_params=_cparams("parallel", "arbitrary"),
        name="ffn",
    )(h, g, w_up, w_up, w_down)


def _inproj_kernel(h_ref, g_ref, w_ref, wg_ref, o_ref, og_ref, xn_ref):
    @pl.when(pl.program_id(1) == 0)
    def _():
        xn = _rms(h_ref[...], g_ref[...]).astype(BF16)
        xn_ref[...] = xn
        og_ref[...] = _dot(xn, wg_ref[...])

    o_ref[...] = _dot(xn_ref[...], w_ref[...])


def _inproj(h, g, w_main, w_gate, tm):
    t = h.shape[0]
    return pl.pallas_call(
        _inproj_kernel,
        out_shape=(jax.ShapeDtypeStruct((t, MAIN_WIDTH), F32),
                   jax.ShapeDtypeStruct((t, LANES), F32)),
        grid=(t // tm, MAIN_WIDTH // IN_TILE),
        in_specs=[
            pl.BlockSpec((tm, D_MODEL), lambda i, j: (i, 0)),
            pl.BlockSpec((1, D_MODEL), lambda i, j: (0, 0)),
            pl.BlockSpec((D_MODEL, IN_TILE), lambda i, j: (0, j)),
            pl.BlockSpec((D_MODEL, LANES), lambda i, j: (0, 0)),
        ],
        out_specs=(pl.BlockSpec((tm, IN_TILE), lambda i, j: (i, j)),
                   pl.BlockSpec((tm, LANES), lambda i, j: (i, 0))),
        scratch_shapes=[pltpu.VMEM((tm, D_MODEL), BF16)],
        compiler_params=_cparams("parallel", "arbitrary"),
        name="inproj",
    )(h, g, w_main, w_gate)


def _merge_kernel(h_ref, ga_ref, gb_ref, gc_ref, ha_ref, hb_ref, hc_ref, w_ref, o_ref):
    merged = (jax.nn.sigmoid(ga_ref[...]) * ha_ref[...]
              + jax.nn.sigmoid(gb_ref[...]) * hb_ref[...]
              + jax.nn.sigmoid(gc_ref[...]) * hc_ref[...])
    o_ref[...] = h_ref[...] + _dot(merged.astype(BF16), w_ref[...])


def _merge(h, proj, h_a, h_b, h_c, w_out, tm):
    t = h.shape[0]
    tm = min(tm, MERGE_TILE)
    row = lambda i: (i, 0)
    col = lambda c: (lambda i: (i, c // N_HEADS))
    return pl.pallas_call(
        _merge_kernel,
        out_shape=jax.ShapeDtypeStruct((t, D_MODEL), F32),
        grid=(t // tm,),
        in_specs=[
            pl.BlockSpec((tm, D_MODEL), row),
            pl.BlockSpec((tm, D_MODEL), col(COL_G_A)),
            pl.BlockSpec((tm, D_MODEL), col(COL_G_B)),
            pl.BlockSpec((tm, D_MODEL), col(COL_G_C)),
            pl.BlockSpec((tm, D_MODEL), row),
            pl.BlockSpec((tm, D_MODEL), row),
            pl.BlockSpec((tm, D_MODEL), row),
            pl.BlockSpec((D_MODEL, D_MODEL), lambda i: (0, 0)),
        ],
        out_specs=pl.BlockSpec((tm, D_MODEL), row),
        compiler_params=_cparams("parallel"),
        name="merge_outproj",
    )(h, proj, proj, proj, h_a, h_b, h_c, w_out)


def _final_norm_kernel(lead, h_ref, g_ref, o_ref):
    o_ref[...] = _rms(h_ref[pl.ds(lead, o_ref.shape[0]), :], g_ref[...])


def _final_norm(h, g, n_seq, seq_len, lead, row_block):
    out_len = seq_len - lead
    return pl.pallas_call(
        functools.partial(_final_norm_kernel, lead),
        out_shape=jax.ShapeDtypeStruct((n_seq * out_len, D_MODEL), F32),
        grid=(n_seq,),
        in_specs=[pl.BlockSpec((seq_len, D_MODEL), lambda b: (b + row_block, 0)),
                  pl.BlockSpec((1, D_MODEL), lambda b: (0, 0))],
        out_specs=pl.BlockSpec((out_len, D_MODEL), lambda b: (b, 0)),
        compiler_params=_cparams("parallel"),
        name="final_norm",
    )(h, g)


def _seq_spec(seq_len, row_block, col):
    return pl.BlockSpec((seq_len, LANES), lambda b, h: (b + row_block, col + h))


def _head_spec(seq_len, row_block):
    return pl.BlockSpec((seq_len, LANES), lambda b, h: (b + row_block, h))


def _state_spec(*tail):
    return pl.BlockSpec((None, None) + tail, lambda b, h: (b, h) + (0,) * len(tail))


def _chunk_plan(seq_len, lead, chunk):
    main = seq_len - lead
    size = min(chunk, main)
    assert main % size == 0
    return size, main // size


def _for_chunks(n, body, init):
    if n == 1:
        return body(0, init)
    return lax.fori_loop(0, n, body, init)


def _aligned(x, m):
    return x if isinstance(x, int) else pl.multiple_of(x, m)


def _head_norm(o, g):
    return o * lax.rsqrt(jnp.mean(o * o, axis=-1, keepdims=True) + EPS) * g


def _hgrn_chunk(st, q, af, iv, lb):
    c = q.shape[0]
    lbv, log_lb, log_1m_lb = lb[0:1], lb[1:2], lb[2:3]
    x = log_1m_lb + _log_sigmoid(af)
    logf = jnp.maximum(log_lb, x) + _softplus_neg_abs(log_lb - x)
    k = (1.0 - lbv) * jax.nn.sigmoid(-af)

    tril = (_iota2((c, c), 0) >= _iota2((c, c), 1)).astype(BF16)
    hi, lo = _split_bf16(logf)
    cum2 = _dot(tril, jnp.concatenate([hi, lo], axis=1))
    cum = cum2[:, :LANES] + cum2[:, LANES:]

    st_bf = st.astype(BF16)
    inter = _dot_nt((q * jnp.exp(cum)).astype(BF16), st_bf)

    sub = min(HGRN_SUB, c)
    iv_bf = iv.astype(BF16)
    parts = []
    for blk in range(c // sub):
        r0 = blk * sub
        n_keys = r0 + sub
        base = cum[r0 - 1:r0] if blk > 0 else jnp.zeros((1, LANES), F32)
        qh = q[r0:n_keys] * jnp.exp(cum[r0:n_keys] - base)
        kh = k[:n_keys] * jnp.exp(base - cum[:n_keys])
        sc = _dot_nt(qh.astype(BF16), kh.astype(BF16))
        causal = _iota2((sub, n_keys), 1) <= _iota2((sub, n_keys), 0) + r0
        sc = jnp.where(causal, sc, 0.0)
        parts.append(_dot(sc.astype(BF16), iv_bf[:n_keys]))
    o = inter + (jnp.concatenate(parts, axis=0) if len(parts) > 1 else parts[0])

    last = cum[c - 1:c]
    kd = k * jnp.exp(last - cum)
    st_new = st * jnp.exp(last) + _dot_tn(iv_bf, kd.astype(BF16))
    return st_new, o


def _hgrn_kernel(seq_len, lead, has_state, *refs):
    if has_state:
        q_ref, f_ref, i_ref, g_ref, lb_ref, gn_ref, s0_ref, _, o_ref, s_ref = refs
        st = s0_ref[...].T
    else:
        q_ref, f_ref, i_ref, g_ref, lb_ref, gn_ref, o_ref, s_ref = refs
        st = jnp.zeros((D_HEAD, D_HEAD), F32)
    lb = lb_ref[...]
    gn = gn_ref[...]
    size, n_main = _chunk_plan(seq_len, lead, HGRN_CHUNK)

    def run(st, rows):
        st, o = _hgrn_chunk(st, q_ref[rows, :], f_ref[rows, :], i_ref[rows, :], lb)
        gate = g_ref[rows, :]
        o_ref[rows, :] = _head_norm(o, gn) * (gate * jax.nn.sigmoid(gate))
        return st

    if lead > 0:
        st = run(st, pl.ds(0, lead))
    st = _for_chunks(n_main, lambda ci, st: run(st, pl.ds(_aligned(lead + ci * size, 8), size)), st)
    s_ref[...] = st.T


def _hgrn(proj, lb_pack, g_norm, n_seq, seq_len, lead, row_block, state=None, out_buf=None):
    t = proj.shape[0]
    has_state = state is not None
    in_specs = [
        _seq_spec(seq_len, row_block, COL_A_Q),
        _seq_spec(seq_len, row_block, COL_A_F),
        _seq_spec(seq_len, row_block, COL_A_I),
        _seq_spec(seq_len, row_block, COL_A_G),
        pl.BlockSpec((3, LANES), lambda b, h: (0, h)),
        pl.BlockSpec((1, LANES), lambda b, h: (0, h)),
    ]
    args = [proj, proj, proj, proj, lb_pack, g_norm]
    aliases = {}
    if has_state:
        in_specs += [_state_spec(D_HEAD, D_HEAD), pl.BlockSpec(memory_space=pl.ANY)]
        args += [state, out_buf]
        aliases = {len(args) - 1: 0}
    return pl.pallas_call(
        functools.partial(_hgrn_kernel, seq_len, lead, has_state),
        out_shape=(jax.ShapeDtypeStruct((t, D_MODEL), F32),
                   jax.ShapeDtypeStruct((n_seq, N_HEADS, D_HEAD, D_HEAD), F32)),
        grid=(n_seq, N_HEADS),
        in_specs=in_specs,
        out_specs=(_head_spec(seq_len, row_block), _state_spec(D_HEAD, D_HEAD)),
        input_output_aliases=aliases,
        compiler_params=_cparams("parallel", "parallel"),
        name="hgrn2",
    )(*args)


def _mlstm_chunk(cst, n, m, q, ks, v, ig_col, lf_col, ig_row, lf_row):
    c = q.shape[0]
    rows, cols = _iota2((c, c), 0), _iota2((c, c), 1)
    causal = rows >= cols
    b_col = jnp.sum(jnp.where(causal, lf_row, 0.0), axis=1, keepdims=True)
    b_row = jnp.sum(jnp.where(rows <= cols, lf_col, 0.0), axis=0, keepdims=True)
    log_d = jnp.where(causal, b_col - b_row + ig_row, -jnp.inf)
    m_t = jnp.maximum(b_col + m, jnp.max(log_d, axis=1, keepdims=True))
    w = jnp.exp(log_d - m_t)
    inter = jnp.exp(b_col + m - m_t)
    q_bf, k_bf, v_bf = q.astype(BF16), ks.astype(BF16), v.astype(BF16)
    ws = w * _dot_nt(q_bf, k_bf)
    num = inter * _dot(q_bf, cst.astype(BF16)) + _dot(ws.astype(BF16), v_bf)
    den = inter * jnp.sum(q * n, axis=1, keepdims=True) + jnp.sum(ws, axis=1, keepdims=True)
    h = num / jnp.maximum(jnp.abs(den), jnp.exp(-m_t))
    m_new = m_t[c - 1:c]
    b_last = b_col[c - 1:c]
    wk = jnp.exp(b_last - b_col + ig_col - m_new)
    carry = jnp.exp(b_last + m - m_new)
    kd = ks * wk
    c_new = carry * cst + _dot_tn(kd.astype(BF16), v_bf)
    n_new = carry * n + jnp.sum(kd, axis=0, keepdims=True)
    return c_new, n_new, m_new, h


def _mlstm_kernel(seq_len, lead, has_state, *refs):
    bias_ref, q_ref, k_ref, v_ref, bo_ref, gc_ref, gn_ref = refs[:7]
    refs = refs[7:]
    if lead > 0:
        grl_ref, refs = refs[0], refs[1:]
    grm_ref, refs = refs[0], refs[1:]
    if has_state:
        c0_ref, n0_ref, m0_ref, _, o_ref, c_ref, n_ref, m_ref = refs
        state = (c0_ref[...], n0_ref[...], m0_ref[:, 0:1])
    else:
        o_ref, c_ref, n_ref, m_ref = refs
        state = (jnp.zeros((D_HEAD, D_HEAD), F32), jnp.zeros((1, D_HEAD), F32), jnp.zeros((1, 1), F32))
    head = pl.program_id(1)
    b_ig = bias_ref[head]
    b_fg = bias_ref[N_HEADS + head]
    gn = gn_ref[...]
    size, n_main = _chunk_plan(seq_len, lead, MLSTM_CHUNK)

    def run(state, rows, gate_rows):
        gates = gc_ref[rows, :]
        lane = _iota2(gates.shape, 1)
        ig_col = jnp.sum(jnp.where(lane == head, gates, 0.0), axis=1, keepdims=True) + b_ig
        fg_col = jnp.sum(jnp.where(lane == head + N_HEADS, gates, 0.0), axis=1, keepdims=True) + b_fg
        ig_row = gate_rows[0:1] + b_ig
        fg_row = gate_rows[1:2] + b_fg
        c_new, n_new, m_new, h = _mlstm_chunk(
            *state, q_ref[rows, :], k_ref[rows, :] * (D_HEAD ** -0.5), v_ref[rows, :],
            ig_col, _log_sigmoid(fg_col), ig_row, _log_sigmoid(fg_row))
        o_ref[rows, :] = _head_norm(h, gn) * jax.nn.sigmoid(bo_ref[rows, :])
        return c_new, n_new, m_new

    if lead > 0:
        state = run(state, pl.ds(0, lead), grl_ref[...])

    def body(ci, state):
        off = _aligned(ci * size, size)
        return run(state, pl.ds(_aligned(lead + ci * size, 8), size), grm_ref[:, pl.ds(off, size)])

    state = _for_chunks(n_main, body, state)
    c_ref[...] = state[0]
    n_ref[...] = state[1]
    m_ref[...] = jnp.broadcast_to(state[2], (1, LANES))


def _mlstm(proj, gates, gate_rows_lead, gate_rows_main, bias, g_norm, n_seq, seq_len, lead, row_block,
           state=None, out_buf=None):
    t = proj.shape[0]
    has_state = state is not None
    in_specs = [
        pl.BlockSpec(memory_space=pltpu.SMEM),
        _seq_spec(seq_len, row_block, COL_B_Q),
        _seq_spec(seq_len, row_block, COL_B_K),
        _seq_spec(seq_len, row_block, COL_B_V),
        _seq_spec(seq_len, row_block, COL_B_O),
        pl.BlockSpec((seq_len, LANES), lambda b, h: (b + row_block, 0)),
        pl.BlockSpec((1, LANES), lambda b, h: (0, h)),
    ]
    args = [bias, proj, proj, proj, proj, gates, g_norm]
    if lead > 0:
        in_specs.append(_state_spec(2, lead))
        args.append(gate_rows_lead)
    in_specs.append(_state_spec(2, seq_len - lead))
    args.append(gate_rows_main)
    aliases = {}
    if has_state:
        in_specs += [_state_spec(D_HEAD, D_HEAD), _state_spec(1, D_HEAD), _state_spec(1, LANES),
                     pl.BlockSpec(memory_space=pl.ANY)]
        args += [state[0], state[1], state[2], out_buf]
        aliases = {len(args) - 1: 0}
    return pl.pallas_call(
        functools.partial(_mlstm_kernel, seq_len, lead, has_state),
        out_shape=(jax.ShapeDtypeStruct((t, D_MODEL), F32),
                   jax.ShapeDtypeStruct((n_seq, N_HEADS, D_HEAD, D_HEAD), F32),
                   jax.ShapeDtypeStruct((n_seq, N_HEADS, 1, D_HEAD), F32),
                   jax.ShapeDtypeStruct((n_seq, N_HEADS, 1, LANES), F32)),
        grid=(n_seq, N_HEADS),
        in_specs=in_specs,
        out_specs=(_head_spec(seq_len, row_block), _state_spec(D_HEAD, D_HEAD),
                   _state_spec(1, D_HEAD), _state_spec(1, LANES)),
        input_output_aliases=aliases,
        compiler_params=_cparams("parallel", "parallel"),
        name="mlstm",
    )(*args)


def _sb_block(qb, kb, vb, bias, upper, mask, carry):
    z = _dot_nt(qb, kb) + bias
    sp = _softplus_neg_abs(z)
    log_beta = jnp.minimum(z, 0.0) - sp
    log_keep = jnp.minimum(-z, 0.0) - sp
    if mask is not None:
        log_keep = jnp.where(mask, log_keep, 0.0)
    hi, lo = _split_bf16(log_keep)
    after = _dot(hi, upper) + _dot(lo, upper)
    w = jnp.exp(log_beta + after + carry)
    if mask is not None:
        w = jnp.where(mask, w, 0.0)
    out = _dot(w.astype(BF16), vb)
    return out, carry + jnp.sum(log_keep, axis=1, keepdims=True)


def _upper(n):
    return (_iota2((n, n), 0) > _iota2((n, n), 1)).astype(BF16)


def _sb_prompt_kernel(seq_len, lead, bias_ref, q_ref, k_ref, v_ref, o_ref, q_bf, k_bf, v_bf):
    bias = bias_ref[pl.program_id(1)]
    q_bf[...] = (q_ref[...] * (D_HEAD ** -0.5)).astype(BF16)
    k_bf[...] = k_ref[...].astype(BF16)
    v_bf[...] = v_ref[...].astype(BF16)
    blk = SB_BLOCK
    n_main = (seq_len - lead) // blk
    upper = _upper(blk)
    strict = _iota2((blk, blk), 1) < _iota2((blk, blk), 0)
    zero_carry = jnp.zeros((blk, 1), F32)

    if lead > 0:
        lead_rows = pl.ds(0, lead)
        lead_mask = _iota2((lead, lead), 1) < _iota2((lead, lead), 0)
        out, _ = _sb_block(q_bf[lead_rows, :], k_bf[lead_rows, :], v_bf[lead_rows, :], bias,
                           _upper(lead), lead_mask, jnp.zeros((lead, 1), F32))
        o_ref[lead_rows, :] = out

    def rows_of(j):
        return pl.ds(pl.multiple_of(lead + j * blk, 16), blk)

    def query_block(i, _):
        qb = q_bf[rows_of(i), :]
        acc, carry = _sb_block(qb, k_bf[rows_of(i), :], v_bf[rows_of(i), :], bias, upper, strict, zero_carry)

        def key_block(jj, state):
            acc, carry = state
            rows = rows_of(i - 1 - jj)
            out, carry = _sb_block(qb, k_bf[rows, :], v_bf[rows, :], bias, upper, None, carry)
            return acc + out, carry

        acc, carry = lax.fori_loop(0, i, key_block, (acc, carry))
        if lead > 0:
            out, _ = _sb_block(qb, k_bf[lead_rows, :], v_bf[lead_rows, :], bias, _upper(lead), None, carry)
            acc = acc + out
        o_ref[rows_of(i), :] = acc
        return 0

    lax.fori_loop(0, n_main, query_block, 0)


def _sb_prompt(proj, bias, n_seq, seq_len, lead):
    t = proj.shape[0]
    return pl.pallas_call(
        functools.partial(_sb_prompt_kernel, seq_len, lead),
        out_shape=jax.ShapeDtypeStruct((t, D_MODEL), F32),
        grid=(n_seq, N_HEADS),
        in_specs=[
            pl.BlockSpec(memory_space=pltpu.SMEM),
            _seq_spec(seq_len, 0, COL_C_Q),
            _seq_spec(seq_len, 0, COL_C_K),
            _seq_spec(seq_len, 0, COL_C_V),
        ],
        out_specs=_head_spec(seq_len, 0),
        scratch_shapes=[pltpu.VMEM((seq_len, D_HEAD), BF16)] * 3,
        compiler_params=_cparams("parallel", "parallel"),
        name="sb_prompt",
    )(bias, proj, proj, proj)


def _sb_sample_kernel(n_new, pt_ref, bias_ref, q_ref, k_ref, v_ref, kp_ref, vp_ref, _, o_ref,
                      q_bf, kn_ref, vn_ref, acc_ref, carry_ref):
    del pt_ref
    p = pl.program_id(1)
    page = kp_ref.shape[0]
    rows = SB_SAMPLE_ROWS
    upper = _upper(page)

    def heads(fn):
        for h in range(N_HEADS):
            fn(h, slice(h * D_HEAD, (h + 1) * D_HEAD))

    @pl.when(p == 0)
    def _():
        q_bf[...] = jnp.zeros_like(q_bf)
        q_bf[0:n_new, :] = q_ref[...] * (D_HEAD ** -0.5)
        kn_ref[...] = jnp.zeros_like(kn_ref)
        vn_ref[...] = jnp.zeros_like(vn_ref)
        kn_ref[0:n_new, :] = k_ref[...]
        vn_ref[0:n_new, :] = v_ref[...]
        mask = _iota2((rows, page), 1) < _iota2((rows, page), 0)

        def new_tokens(h, cols):
            out, carry = _sb_block(q_bf[:, cols].astype(BF16), kn_ref[:, cols].astype(BF16),
                                   vn_ref[:, cols].astype(BF16), bias_ref[h], upper, mask,
                                   jnp.zeros((rows, page), F32))
            acc_ref[:, cols] = out
            carry_ref[h] = carry

        heads(new_tokens)

    def one_page(h, cols):
        out, carry = _sb_block(q_bf[:, cols].astype(BF16), kp_ref[:, h, :].astype(BF16),
                               vp_ref[:, h, :].astype(BF16), bias_ref[h], upper, None, carry_ref[h])
        acc_ref[:, cols] += out
        carry_ref[h] = carry

    heads(one_page)

    @pl.when(p == pl.num_programs(1) - 1)
    def _():
        o_ref[...] = acc_ref[0:n_new, :]


def _sb_sample(proj, bias, cache_k, cache_v, page_table, layer, n_seq, n_new, row_block, out_buf):
    t = proj.shape[0]
    n_pages = page_table.shape[1]
    page = cache_k.shape[2]
    assert n_new <= SB_SAMPLE_ROWS <= page
    tok = lambda col: pl.BlockSpec((n_new, D_MODEL), lambda b, p, pt: (b + row_block, col // N_HEADS))
    paged = pl.BlockSpec((None, None, page, N_HEADS, D_HEAD),
                         lambda b, p, pt: (pt[b, n_pages - 1 - p], layer, 0, 0, 0))
    return pl.pallas_call(
        functools.partial(_sb_sample_kernel, n_new),
        out_shape=jax.ShapeDtypeStruct((t, D_MODEL), F32),
        grid_spec=pltpu.PrefetchScalarGridSpec(
            num_scalar_prefetch=1,
            grid=(n_seq, n_pages),
            in_specs=[pl.BlockSpec(memory_space=pltpu.SMEM), tok(COL_C_Q), tok(COL_C_K), tok(COL_C_V),
                      paged, paged, pl.BlockSpec(memory_space=pl.ANY)],
            out_specs=pl.BlockSpec((n_new, D_MODEL), lambda b, p, pt: (b + row_block, 0)),
            scratch_shapes=[pltpu.VMEM((SB_SAMPLE_ROWS, D_MODEL), F32),
                            pltpu.VMEM((page, D_MODEL), F32), pltpu.VMEM((page, D_MODEL), F32),
                            pltpu.VMEM((SB_SAMPLE_ROWS, D_MODEL), F32),
                            pltpu.VMEM((N_HEADS, SB_SAMPLE_ROWS, page), F32)],
        ),
        input_output_aliases={7: 0},
        compiler_params=_cparams("parallel", "arbitrary"),
        name="sb_sample",
    )(page_table, bias, proj, proj, proj, cache_k, cache_v, out_buf)


def _token_tile(t):
    for tm in (1280, 1024, 512, 256, 128, 64, 32, 16, 8):
        if t % tm == 0:
            return tm
    raise ValueError(f"token count {t} is not a multiple of 8")


def _gate_rows(gates, n_seq, seq_len):
    g = gates[:, :2 * N_HEADS].reshape(n_seq, seq_len, 2, N_HEADS)
    return jnp.transpose(g, (0, 3, 2, 1))


def kernel(x_prompt, x_sample, state_hgrn, state_mlstm_C, state_mlstm_n, state_mlstm_m, cache_k, cache_v, page_table, meta_tokens, g_ffn1, w_ffn1_up, w_ffn1_down, g_mix, w_in, lb_param, b_ig, b_fg, b_sb, g_norm_a, g_norm_b, w_out, g_ffn2, w_ffn2_up, w_ffn2_down, g_final):
    depth = w_in.shape[0]
    n_p, s_p, _ = x_prompt.shape
    n_s, l_s, _ = x_sample.shape
    l_p = s_p + N_META
    t_p, t_s = n_p * l_p, n_s * l_s
    t = t_p + t_s
    tm = _token_tile(t)
    rb_s = t_p // l_s
    assert t_p % l_s == 0

    meta = jnp.broadcast_to(meta_tokens.astype(x_prompt.dtype)[None], (n_p, N_META, D_MODEL))
    h = jnp.concatenate([jnp.concatenate([meta, x_prompt], axis=1).reshape(t_p, D_MODEL),
                         x_sample.reshape(t_s, D_MODEL)], axis=0)

    lb = jnp.cumsum(jax.nn.softmax(lb_param.astype(F32), axis=0), axis=0)
    lb = lb - lb[0]
    lb_pack = jnp.stack([lb, jnp.log(lb), jnp.log1p(-lb)], axis=1)

    w_main = jnp.concatenate([w_in[:, :, :GATE_OFFSET], w_in[:, :, GATE_OFFSET + 2 * N_HEADS:]], axis=2).astype(BF16)
    w_gate = jnp.pad(w_in[:, :, GATE_OFFSET:GATE_OFFSET + 2 * N_HEADS],
                     ((0, 0), (0, 0), (0, LANES - 2 * N_HEADS))).astype(BF16)
    up1, down1 = w_ffn1_up.astype(BF16), w_ffn1_down.astype(BF16)
    up2, down2 = w_ffn2_up.astype(BF16), w_ffn2_down.astype(BF16)
    w_o = w_out.astype(BF16)
    gate_bias = jnp.concatenate([b_ig, b_fg], axis=1).astype(F32)
    m0 = jnp.broadcast_to(state_mlstm_m[..., None, None], state_mlstm_m.shape + (1, LANES))
    n0 = state_mlstm_n[:, :, :, None, :]

    outs = {k: [] for k in ("pS", "pC", "pn", "pm", "pk", "pv", "sS", "sC", "sn", "sm", "sk", "sv")}
    for l in range(depth):
        row = lambda a: a[l][None].astype(F32)
        h = _ffn(h, row(g_ffn1), up1[l], down1[l], tm)
        proj, gates = _inproj(h, row(g_mix), w_main[l], w_gate[l], tm)

        gn_a, gn_b = row(g_norm_a), row(g_norm_b)
        h_a, s_p_l = _hgrn(proj, lb_pack[l], gn_a, n_p, l_p, N_META, 0)
        h_a, s_s_l = _hgrn(proj, lb_pack[l], gn_a, n_s, l_s, 0, rb_s, state=state_hgrn[l], out_buf=h_a)

        gr_p = _gate_rows(gates[:t_p], n_p, l_p)
        gr_s = _gate_rows(gates[t_p:], n_s, l_s)
        h_b, c_p, nn_p, m_p = _mlstm(proj, gates, gr_p[..., :N_META], gr_p[..., N_META:], gate_bias[l], gn_b,
                                     n_p, l_p, N_META, 0)
        h_b, c_s, nn_s, m_s = _mlstm(proj, gates, None, gr_s, gate_bias[l], gn_b, n_s, l_s, 0, rb_s,
                                     state=(state_mlstm_C[l], n0[l], m0[l]), out_buf=h_b)

        sb_bias = b_sb[l].astype(F32)
        h_c = _sb_prompt(proj, sb_bias, n_p, l_p, N_META)
        h_c = _sb_sample(proj, sb_bias, cache_k, cache_v, page_table, l, n_s, l_s, rb_s, h_c)

        h = _merge(h, proj, h_a, h_b, h_c, w_o[l], tm)
        h = _ffn(h, row(g_ffn2), up2[l], down2[l], tm)

        k_c = proj[:, COL_C_K * LANES:(COL_C_K + N_HEADS) * LANES]
        v_c = proj[:, COL_C_V * LANES:(COL_C_V + N_HEADS) * LANES]
        outs["pS"].append(s_p_l)
        outs["pC"].append(c_p)
        outs["pn"].append(nn_p[:, :, 0])
        outs["pm"].append(m_p[:, :, 0, 0])
        outs["pk"].append(k_c[:t_p].reshape(n_p, l_p, N_HEADS, D_HEAD))
        outs["pv"].append(v_c[:t_p].reshape(n_p, l_p, N_HEADS, D_HEAD))
        outs["sS"].append(s_s_l)
        outs["sC"].append(c_s)
        outs["sn"].append(nn_s[:, :, 0])
        outs["sm"].append(m_s[:, :, 0, 0])
        outs["sk"].append(k_c[t_p:].reshape(n_s, l_s, N_HEADS, D_HEAD))
        outs["sv"].append(v_c[t_p:].reshape(n_s, l_s, N_HEADS, D_HEAD))

    g_fin = g_final[None].astype(F32)
    y_prompt = _final_norm(h, g_fin, n_p, l_p, N_META, 0).reshape(n_p, s_p, D_MODEL)
    y_sample = _final_norm(h, g_fin, n_s, l_s, 0, rb_s).reshape(n_s, l_s, D_MODEL)
    st = lambda k, axis=0: jnp.stack(outs[k], axis=axis)
    return (y_prompt, y_sample, st("pS"), st("pC"), st("pn"), st("pm"), st("pk", 1), st("pv", 1),
            st("sS"), st("sC"), st("sn"), st("sm"), st("sk", 1), st("sv", 1))
```

```python
import functools

import numpy as np
import jax
import jax.numpy as jnp
from jax import lax
from jax.experimental import pallas as pl
from jax.experimental.pallas import tpu as pltpu

F32 = jnp.float32
BF16 = jnp.bfloat16

D_MODEL = 1024
N_HEADS = 8
D_HEAD = 128
D_FF = 2816
N_META = 16
EPS = 1e-6

LANES = 128
VMEM_LIMIT_BYTES = 56 * 1024 * 1024

COL_A_Q, COL_A_F, COL_A_I, COL_A_G = 0, 8, 16, 24
COL_B_Q, COL_B_K, COL_B_V, COL_B_O = 32, 40, 48, 56
COL_C_Q, COL_C_K, COL_C_V = 64, 72, 80
COL_G_A, COL_G_B, COL_G_C = 88, 96, 104
MAIN_WIDTH = 112 * LANES
GATE_OFFSET = 8 * D_MODEL

HGRN_CHUNK = 128
HGRN_SUB = 16
MLSTM_CHUNK = 128
SB_BLOCK = 256
SB_PAGES_PER_STEP = 8

FF_TILE = 256
IN_TILE = 1024
MERGE_TILE = 256
SB_SAMPLE_ROWS = 16


def _cparams(*sem):
    return pltpu.CompilerParams(dimension_semantics=sem, vmem_limit_bytes=VMEM_LIMIT_BYTES)


def _dot(a, b):
    return jnp.dot(a, b, preferred_element_type=F32)


def _dot_nt(a, b):
    return lax.dot_general(a, b, (((1,), (1,)), ((), ())), preferred_element_type=F32)


def _dot_tn(a, b):
    return lax.dot_general(a, b, (((0,), (0,)), ((), ())), preferred_element_type=F32)


def _rms(x, g):
    return x * lax.rsqrt(jnp.mean(x * x, axis=-1, keepdims=True) + EPS) * g


def _softplus_neg_abs(z):
    return jnp.log(1.0 + jnp.exp(-jnp.abs(z)))


def _log_sigmoid(z):
    return jnp.minimum(z, 0.0) - _softplus_neg_abs(z)


def _split_bf16(x):
    hi = x.astype(BF16)
    lo = (x - hi.astype(F32)).astype(BF16)
    return hi, lo


def _iota2(shape, dim):
    return lax.broadcasted_iota(jnp.int32, shape, dim)


def _ffn_kernel(h_ref, g_ref, wg_ref, wu_ref, wd_ref, o_ref, xn_ref, acc_ref):
    f = pl.program_id(1)

    @pl.when(f == 0)
    def _():
        xn_ref[...] = _rms(h_ref[...], g_ref[...]).astype(BF16)
        acc_ref[...] = jnp.zeros_like(acc_ref)

    xn = xn_ref[...]
    gate = _dot(xn, wg_ref[...])
    up = _dot(xn, wu_ref[...])
    act = (gate * jax.nn.sigmoid(gate) * up).astype(BF16)
    acc_ref[...] += _dot(act, wd_ref[...])

    @pl.when(f == pl.num_programs(1) - 1)
    def _():
        o_ref[...] = h_ref[...] + 0.5 * acc_ref[...]


def _ffn(h, g, w_up, w_down, tm):
    t = h.shape[0]
    nf = D_FF // FF_TILE
    return pl.pallas_call(
        _ffn_kernel,
        out_shape=jax.ShapeDtypeStruct((t, D_MODEL), F32),
        grid=(t // tm, nf),
        in_specs=[
            pl.BlockSpec((tm, D_MODEL), lambda i, f: (i, 0)),
            pl.BlockSpec((1, D_MODEL), lambda i, f: (0, 0)),
            pl.BlockSpec((D_MODEL, FF_TILE), lambda i, f: (0, f)),
            pl.BlockSpec((D_MODEL, FF_TILE), lambda i, f: (0, f + D_FF // FF_TILE)),
            pl.BlockSpec((FF_TILE, D_MODEL), lambda i, f: (f, 0)),
        ],
        out_specs=pl.BlockSpec((tm, D_MODEL), lambda i, f: (i, 0)),
        scratch_shapes=[pltpu.VMEM((tm, D_MODEL), BF16), pltpu.VMEM((tm, D_MODEL), F32)],
        compiler_params=_cparams("parallel", "arbitrary"),
        name="ffn",
    )(h, g, w_up, w_up, w_down)


def _inproj_kernel(h_ref, g_ref, w_ref, wg_ref, o_ref, og_ref, xn_ref):
    @pl.when(pl.program_id(1) == 0)
    def _():
        xn = _rms(h_ref[...], g_ref[...]).astype(BF16)
        xn_ref[...] = xn
        og_ref[...] = _dot(xn, wg_ref[...])

    o_ref[...] = _dot(xn_ref[...], w_ref[...])


def _inproj(h, g, w_main, w_gate, tm):
    t = h.shape[0]
    return pl.pallas_call(
        _inproj_kernel,
        out_shape=(jax.ShapeDtypeStruct((t, MAIN_WIDTH), F32),
                   jax.ShapeDtypeStruct((t, LANES), F32)),
        grid=(t // tm, MAIN_WIDTH // IN_TILE),
        in_specs=[
            pl.BlockSpec((tm, D_MODEL), lambda i, j: (i, 0)),
            pl.BlockSpec((1, D_MODEL), lambda i, j: (0, 0)),
            pl.BlockSpec((D_MODEL, IN_TILE), lambda i, j: (0, j)),
            pl.BlockSpec((D_MODEL, LANES), lambda i, j: (0, 0)),
        ],
        out_specs=(pl.BlockSpec((tm, IN_TILE), lambda i, j: (i, j)),
                   pl.BlockSpec((tm, LANES), lambda i, j: (i, 0))),
        scratch_shapes=[pltpu.VMEM((tm, D_MODEL), BF16)],
        compiler_params=_cparams("parallel", "arbitrary"),
        name="inproj",
    )(h, g, w_main, w_gate)


def _merge_kernel(h_ref, ga_ref, gb_ref, gc_ref, ha_ref, hb_ref, hc_ref, w_ref, o_ref):
    merged = (jax.nn.sigmoid(ga_ref[...]) * ha_ref[...]
              + jax.nn.sigmoid(gb_ref[...]) * hb_ref[...]
              + jax.nn.sigmoid(gc_ref[...]) * hc_ref[...])
    o_ref[...] = h_ref[...] + _dot(merged.astype(BF16), w_ref[...])


def _merge(h, proj, h_a, h_b, h_c, w_out, tm):
    t = h.shape[0]
    tm = min(tm, MERGE_TILE)
    row = lambda i: (i, 0)
    col = lambda c: (lambda i: (i, c // N_HEADS))
    return pl.pallas_call(
        _merge_kernel,
        out_shape=jax.ShapeDtypeStruct((t, D_MODEL), F32),
        grid=(t // tm,),
        in_specs=[
            pl.BlockSpec((tm, D_MODEL), row),
            pl.BlockSpec((tm, D_MODEL), col(COL_G_A)),
            pl.BlockSpec((tm, D_MODEL), col(COL_G_B)),
            pl.BlockSpec((tm, D_MODEL), col(COL_G_C)),
            pl.BlockSpec((tm, D_MODEL), row),
            pl.BlockSpec((tm, D_MODEL), row),
            pl.BlockSpec((tm, D_MODEL), row),
            pl.BlockSpec((D_MODEL, D_MODEL), lambda i: (0, 0)),
        ],
        out_specs=pl.BlockSpec((tm, D_MODEL), row),
        compiler_params=_cparams("parallel"),
        name="merge_outproj",
    )(h, proj, proj, proj, h_a, h_b, h_c, w_out)


def _final_norm_kernel(lead, h_ref, g_ref, o_ref):
    o_ref[...] = _rms(h_ref[pl.ds(lead, o_ref.shape[0]), :], g_ref[...])


def _final_norm(h, g, n_seq, seq_len, lead, row_block):
    out_len = seq_len - lead
    return pl.pallas_call(
        functools.partial(_final_norm_kernel, lead),
        out_shape=jax.ShapeDtypeStruct((n_seq * out_len, D_MODEL), F32),
        grid=(n_seq,),
        in_specs=[pl.BlockSpec((seq_len, D_MODEL), lambda b: (b + row_block, 0)),
                  pl.BlockSpec((1, D_MODEL), lambda b: (0, 0))],
        out_specs=pl.BlockSpec((out_len, D_MODEL), lambda b: (b, 0)),
        compiler_params=_cparams("parallel"),
        name="final_norm",
    )(h, g)


def _seq_spec(seq_len, row_block, col):
    return pl.BlockSpec((seq_len, LANES), lambda b, h: (b + row_block, col + h))


def _head_spec(seq_len, row_block):
    return pl.BlockSpec((seq_len, LANES), lambda b, h: (b + row_block, h))


def _state_spec(*tail):
    return pl.BlockSpec((None, None) + tail, lambda b, h: (b, h) + (0,) * len(tail))


def _chunk_plan(seq_len, lead, chunk):
    main = seq_len - lead
    size = min(chunk, main)
    assert main % size == 0
    return size, main // size


def _for_chunks(n, body, init):
    if n == 1:
        return body(0, init)
    return lax.fori_loop(0, n, body, init)


def _aligned(x, m):
    return x if isinstance(x, int) else pl.multiple_of(x, m)


def _head_norm(o, g):
    return o * lax.rsqrt(jnp.mean(o * o, axis=-1, keepdims=True) + EPS) * g


def _hgrn_levels(c):
    sizes, size = [], 2 * HGRN_SUB
    while size <= c:
        sizes.append(size)
        size *= 2
    assert not sizes or sizes[-1] == c
    return sizes


def _hgrn_cum_matrix(c):
    t = np.arange(c)[:, None]
    s = np.arange(c)[None, :]
    mats = [(s <= t).astype(np.float32)]
    if c > HGRN_SUB:
        mats.append(((s <= t) & (t // HGRN_SUB == s // HGRN_SUB)).astype(np.float32))
        for size in _hgrn_levels(c):
            mid = (t // size) * size + size // 2 - 1
            mats.append((s <= t).astype(np.float32) - (s <= mid).astype(np.float32))
    return jnp.asarray(np.concatenate(mats, axis=0), BF16)


def _hgrn_masks(c):
    rows, cols = _iota2((c, c), 0), _iota2((c, c), 1)
    diag = cols <= rows
    if c <= HGRN_SUB:
        return diag, []
    shift = HGRN_SUB.bit_length() - 1
    diag = diag & ((rows >> shift) == (cols >> shift))
    levels = []
    for size in _hgrn_levels(c):
        shift = size.bit_length() - 1
        half = size // 2
        levels.append(((rows >> shift) == (cols >> shift)) & ((rows & half) != 0) & ((cols & half) == 0))
    return diag, levels


def _hgrn_chunk(st, q, af, iv, lb, cum_mat, masks):
    c = q.shape[0]
    lbv, log_lb, log_1m_lb = lb[0:1], lb[1:2], lb[2:3]
    x = log_1m_lb + _log_sigmoid(af)
    logf = jnp.maximum(log_lb, x) + _softplus_neg_abs(log_lb - x)
    k = (1.0 - lbv) * jax.nn.sigmoid(-af)

    hi, lo = _split_bf16(logf)
    rel2 = _dot(cum_mat, jnp.concatenate([hi, lo], axis=1))
    rel = rel2[:, :LANES] + rel2[:, LANES:]
    cum = rel[:c]

    inter = _dot_nt((q * jnp.exp(cum)).astype(BF16), st.astype(BF16))

    mask_diag, mask_levels = masks
    d = rel[c:2 * c] if mask_levels else cum
    scores = jnp.where(mask_diag, _dot_nt((q * jnp.exp(d)).astype(BF16), (k * jnp.exp(-d)).astype(BF16)), 0.0)
    for lev, mask in enumerate(mask_levels):
        d = rel[(2 + lev) * c:(3 + lev) * c]
        qh = q * jnp.exp(jnp.minimum(d, 0.0))
        kh = k * jnp.exp(jnp.minimum(-d, 0.0))
        scores += jnp.where(mask, _dot_nt(qh.astype(BF16), kh.astype(BF16)), 0.0)
    iv_bf = iv.astype(BF16)
    o = inter + _dot(scores.astype(BF16), iv_bf)

    last = cum[c - 1:c]
    kd = k * jnp.exp(last - cum)
    st_new = st * jnp.exp(last) + _dot_tn(iv_bf, kd.astype(BF16))
    return st_new, o


def _hgrn_kernel(seq_len, lead, has_state, *refs):
    if has_state:
        q_ref, f_ref, i_ref, g_ref, lb_ref, gn_ref, cm_ref, s0_ref, _, o_ref, s_ref = refs
        st = s0_ref[...].T
    else:
        q_ref, f_ref, i_ref, g_ref, lb_ref, gn_ref, cm_ref, o_ref, s_ref = refs
        st = jnp.zeros((D_HEAD, D_HEAD), F32)
    lb = lb_ref[...]
    gn = gn_ref[...]
    size, n_main = _chunk_plan(seq_len, lead, HGRN_CHUNK)

    def run(st, rows, cum_mat, masks):
        st, o = _hgrn_chunk(st, q_ref[rows, :], f_ref[rows, :], i_ref[rows, :], lb, cum_mat, masks)
        gate = g_ref[rows, :]
        o_ref[rows, :] = _head_norm(o, gn) * (gate * jax.nn.sigmoid(gate))
        return st

    if lead > 0:
        assert lead <= HGRN_SUB
        tril = (_iota2((lead, lead), 0) >= _iota2((lead, lead), 1)).astype(BF16)
        st = run(st, pl.ds(0, lead), tril, _hgrn_masks(lead))
    cum_mat, masks = cm_ref[...], _hgrn_masks(size)
    st = _for_chunks(
        n_main, lambda ci, st: run(st, pl.ds(_aligned(lead + ci * size, 8), size), cum_mat, masks), st)
    s_ref[...] = st.T


def _hgrn(proj, lb_pack, g_norm, n_seq, seq_len, lead, row_block, state=None, out_buf=None):
    t = proj.shape[0]
    has_state = state is not None
    cum_mat = _hgrn_cum_matrix(_chunk_plan(seq_len, lead, HGRN_CHUNK)[0])
    in_specs = [
        _seq_spec(seq_len, row_block, COL_A_Q),
        _seq_spec(seq_len, row_block, COL_A_F),
        _seq_spec(seq_len, row_block, COL_A_I),
        _seq_spec(seq_len, row_block, COL_A_G),
        pl.BlockSpec((3, LANES), lambda b, h: (0, h)),
        pl.BlockSpec((1, LANES), lambda b, h: (0, h)),
        pl.BlockSpec(cum_mat.shape, lambda b, h: (0, 0)),
    ]
    args = [proj, proj, proj, proj, lb_pack, g_norm, cum_mat]
    aliases = {}
    if has_state:
        in_specs += [_state_spec(D_HEAD, D_HEAD), pl.BlockSpec(memory_space=pl.ANY)]
        args += [state, out_buf]
        aliases = {len(args) - 1: 0}
    return pl.pallas_call(
        functools.partial(_hgrn_kernel, seq_len, lead, has_state),
        out_shape=(jax.ShapeDtypeStruct((t, D_MODEL), F32),
                   jax.ShapeDtypeStruct((n_seq, N_HEADS, D_HEAD, D_HEAD), F32)),
        grid=(n_seq, N_HEADS),
        in_specs=in_specs,
        out_specs=(_head_spec(seq_len, row_block), _state_spec(D_HEAD, D_HEAD)),
        input_output_aliases=aliases,
        compiler_params=_cparams("parallel", "parallel"),
        name="hgrn2",
    )(*args)


def _mlstm_chunk(cst, n, m, q, ks, v, ig_col, lf_col, ig_row, lf_row):
    c = q.shape[0]
    rows, cols = _iota2((c, c), 0), _iota2((c, c), 1)
    causal = rows >= cols
    b_col = jnp.sum(jnp.where(causal, lf_row, 0.0), axis=1, keepdims=True)
    b_row = jnp.sum(jnp.where(rows <= cols, lf_col, 0.0), axis=0, keepdims=True)
    log_d = jnp.where(causal, b_col - b_row + ig_row, -jnp.inf)
    m_t = jnp.maximum(b_col + m, jnp.max(log_d, axis=1, keepdims=True))
    w = jnp.exp(log_d - m_t)
    inter = jnp.exp(b_col + m - m_t)
    q_bf, k_bf, v_bf = q.astype(BF16), ks.astype(BF16), v.astype(BF16)
    ws = w * _dot_nt(q_bf, k_bf)
    num = inter * _dot(q_bf, cst.astype(BF16)) + _dot(ws.astype(BF16), v_bf)
    den = inter * jnp.sum(q * n, axis=1, keepdims=True) + jnp.sum(ws, axis=1, keepdims=True)
    h = num / jnp.maximum(jnp.abs(den), jnp.exp(-m_t))
    m_new = m_t[c - 1:c]
    b_last = b_col[c - 1:c]
    wk = jnp.exp(b_last - b_col + ig_col - m_new)
    carry = jnp.exp(b_last + m - m_new)
    kd = ks * wk
    c_new = carry * cst + _dot_tn(kd.astype(BF16), v_bf)
    n_new = carry * n + jnp.sum(kd, axis=0, keepdims=True)
    return c_new, n_new, m_new, h


def _mlstm_kernel(seq_len, lead, has_state, *refs):
    bias_ref, q_ref, k_ref, v_ref, bo_ref, gc_ref, gn_ref = refs[:7]
    refs = refs[7:]
    if lead > 0:
        grl_ref, refs = refs[0], refs[1:]
    grm_ref, refs = refs[0], refs[1:]
    if has_state:
        c0_ref, n0_ref, m0_ref, _, o_ref, c_ref, n_ref, m_ref = refs
        state = (c0_ref[...], n0_ref[...], m0_ref[:, 0:1])
    else:
        o_ref, c_ref, n_ref, m_ref = refs
        state = (jnp.zeros((D_HEAD, D_HEAD), F32), jnp.zeros((1, D_HEAD), F32), jnp.zeros((1, 1), F32))
    head = pl.program_id(1)
    b_ig = bias_ref[head]
    b_fg = bias_ref[N_HEADS + head]
    gn = gn_ref[...]
    size, n_main = _chunk_plan(seq_len, lead, MLSTM_CHUNK)

    def run(state, rows, gate_rows):
        gates = gc_ref[rows, :]
        lane = _iota2(gates.shape, 1)
        ig_col = jnp.sum(jnp.where(lane == head, gates, 0.0), axis=1, keepdims=True) + b_ig
        fg_col = jnp.sum(jnp.where(lane == head + N_HEADS, gates, 0.0), axis=1, keepdims=True) + b_fg
        ig_row = gate_rows[0:1] + b_ig
        fg_row = gate_rows[1:2] + b_fg
        c_new, n_new, m_new, h = _mlstm_chunk(
            *state, q_ref[rows, :], k_ref[rows, :] * (D_HEAD ** -0.5), v_ref[rows, :],
            ig_col, _log_sigmoid(fg_col), ig_row, _log_sigmoid(fg_row))
        o_ref[rows, :] = _head_norm(h, gn) * jax.nn.sigmoid(bo_ref[rows, :])
        return c_new, n_new, m_new

    if lead > 0:
        state = run(state, pl.ds(0, lead), grl_ref[...])

    def body(ci, state):
        off = _aligned(ci * size, size)
        return run(state, pl.ds(_aligned(lead + ci * size, 8), size), grm_ref[:, pl.ds(off, size)])

    state = _for_chunks(n_main, body, state)
    c_ref[...] = state[0]
    n_ref[...] = state[1]
    m_ref[...] = jnp.broadcast_to(state[2], (1, LANES))


def _mlstm(proj, gates, gate_rows_lead, gate_rows_main, bias, g_norm, n_seq, seq_len, lead, row_block,
           state=None, out_buf=None):
    t = proj.shape[0]
    has_state = state is not None
    in_specs = [
        pl.BlockSpec(memory_space=pltpu.SMEM),
        _seq_spec(seq_len, row_block, COL_B_Q),
        _seq_spec(seq_len, row_block, COL_B_K),
        _seq_spec(seq_len, row_block, COL_B_V),
        _seq_spec(seq_len, row_block, COL_B_O),
        pl.BlockSpec((seq_len, LANES), lambda b, h: (b + row_block, 0)),
        pl.BlockSpec((1, LANES), lambda b, h: (0, h)),
    ]
    args = [bias, proj, proj, proj, proj, gates, g_norm]
    if lead > 0:
        in_specs.append(_state_spec(2, lead))
        args.append(gate_rows_lead)
    in_specs.append(_state_spec(2, seq_len - lead))
    args.append(gate_rows_main)
    aliases = {}
    if has_state:
        in_specs += [_state_spec(D_HEAD, D_HEAD), _state_spec(1, D_HEAD), _state_spec(1, LANES),
                     pl.BlockSpec(memory_space=pl.ANY)]
        args += [state[0], state[1], state[2], out_buf]
        aliases = {len(args) - 1: 0}
    return pl.pallas_call(
        functools.partial(_mlstm_kernel, seq_len, lead, has_state),
        out_shape=(jax.ShapeDtypeStruct((t, D_MODEL), F32),
                   jax.ShapeDtypeStruct((n_seq, N_HEADS, D_HEAD, D_HEAD), F32),
                   jax.ShapeDtypeStruct((n_seq, N_HEADS, 1, D_HEAD), F32),
                   jax.ShapeDtypeStruct((n_seq, N_HEADS, 1, LANES), F32)),
        grid=(n_seq, N_HEADS),
        in_specs=in_specs,
        out_specs=(_head_spec(seq_len, row_block), _state_spec(D_HEAD, D_HEAD),
                   _state_spec(1, D_HEAD), _state_spec(1, LANES)),
        input_output_aliases=aliases,
        compiler_params=_cparams("parallel", "parallel"),
        name="mlstm",
    )(*args)


def _sb_block(qb, kb, vb, bias, upper, mask, carry):
    w, carry = _sb_weights(_dot_nt(qb, kb) + bias, upper, mask, carry)
    return _dot(w.astype(BF16), vb), carry


def _sb_weights(z, upper, mask, carry):
    tq = z.shape[0]
    sp = _softplus_neg_abs(z)
    log_beta = jnp.minimum(z, 0.0) - sp
    log_keep = jnp.minimum(-z, 0.0) - sp
    if mask is not None:
        log_keep = jnp.where(mask, log_keep, 0.0)
    hi, lo = _split_bf16(log_keep)
    after2 = _dot(jnp.concatenate([hi, lo], axis=0), upper)
    w = jnp.exp(log_beta + (after2[:tq] + after2[tq:]) + carry)
    if mask is not None:
        w = jnp.where(mask, w, 0.0)
    return w, carry + jnp.sum(log_keep, axis=1, keepdims=True)


def _upper(n):
    return (_iota2((n, n), 0) > _iota2((n, n), 1)).astype(BF16)


def _sb_prompt_kernel(seq_len, lead, bias_ref, q_ref, k_ref, v_ref, o_ref, q_bf, k_bf, v_bf, acc_ref):
    bias = bias_ref[pl.program_id(1)]
    q_bf[...] = (q_ref[...] * (D_HEAD ** -0.5)).astype(BF16)
    k_bf[...] = k_ref[...].astype(BF16)
    v_bf[...] = v_ref[...].astype(BF16)
    blk = SB_BLOCK
    n_pairs = (seq_len - lead) // (2 * blk)
    upper = _upper(blk)
    strict = _iota2((blk, blk), 1) < _iota2((blk, blk), 0)
    zero_carry = jnp.zeros((blk, 1), F32)

    if lead > 0:
        lead_rows = pl.ds(0, lead)
        lead_mask = _iota2((lead, lead), 1) < _iota2((lead, lead), 0)
        out, _ = _sb_block(q_bf[lead_rows, :], k_bf[lead_rows, :], v_bf[lead_rows, :], bias,
                           _upper(lead), lead_mask, jnp.zeros((lead, 1), F32))
        o_ref[lead_rows, :] = out

    def rows_of(j):
        return pl.ds(_aligned(lead + j * blk, 16), blk)

    def attend(qb, rows, mask, carry):
        return _sb_block(qb, k_bf[rows, :], v_bf[rows, :], bias, upper, mask, carry)

    def query_pair(i, _):
        r_lo, r_hi = rows_of(2 * i), rows_of(2 * i + 1)
        q_lo, q_hi = q_bf[r_lo, :], q_bf[r_hi, :]
        acc_lo, c_lo = attend(q_lo, r_lo, strict, zero_carry)
        acc_hi, c_hi = attend(q_hi, r_hi, strict, zero_carry)
        out, c_hi = attend(q_hi, r_lo, None, c_hi)
        acc_ref[0] = acc_lo
        acc_ref[1] = acc_hi + out

        def key_pair(jj, carries):
            c_lo, c_hi = carries
            newer, older = rows_of(2 * (i - 1 - jj) + 1), rows_of(2 * (i - 1 - jj))
            o_lo, c_lo = attend(q_lo, newer, None, c_lo)
            o_hi, c_hi = attend(q_hi, newer, None, c_hi)
            p_lo, c_lo = attend(q_lo, older, None, c_lo)
            p_hi, c_hi = attend(q_hi, older, None, c_hi)
            acc_ref[0] += o_lo + p_lo
            acc_ref[1] += o_hi + p_hi
            return c_lo, c_hi

        c_lo, c_hi = lax.fori_loop(0, i, key_pair, (c_lo, c_hi))
        acc_lo, acc_hi = acc_ref[0], acc_ref[1]
        if lead > 0:
            k_lead, v_lead, up_lead = k_bf[lead_rows, :], v_bf[lead_rows, :], _upper(lead)
            acc_lo += _sb_block(q_lo, k_lead, v_lead, bias, up_lead, None, c_lo)[0]
            acc_hi += _sb_block(q_hi, k_lead, v_lead, bias, up_lead, None, c_hi)[0]
        o_ref[r_lo, :] = acc_lo
        o_ref[r_hi, :] = acc_hi
        return 0

    lax.fori_loop(0, n_pairs, query_pair, 0)


def _sb_prompt(proj, bias, n_seq, seq_len, lead):
    t = proj.shape[0]
    assert (seq_len - lead) % (2 * SB_BLOCK) == 0
    return pl.pallas_call(
        functools.partial(_sb_prompt_kernel, seq_len, lead),
        out_shape=jax.ShapeDtypeStruct((t, D_MODEL), F32),
        grid=(n_seq, N_HEADS),
        in_specs=[
            pl.BlockSpec(memory_space=pltpu.SMEM),
            _seq_spec(seq_len, 0, COL_C_Q),
            _seq_spec(seq_len, 0, COL_C_K),
            _seq_spec(seq_len, 0, COL_C_V),
        ],
        out_specs=_head_spec(seq_len, 0),
        scratch_shapes=[pltpu.VMEM((seq_len, D_HEAD), BF16)] * 3 + [pltpu.VMEM((2, SB_BLOCK, D_HEAD), F32)],
        compiler_params=_cparams("parallel", "parallel"),
        name="sb_prompt",
    )(bias, proj, proj, proj)


def _sb_sample_kernel(n_new, n_step, pt_ref, bias_ref, q_ref, k_ref, v_ref, *refs):
    del pt_ref
    kp_refs, vp_refs = refs[:n_step], refs[n_step:2 * n_step]
    o_ref, q_bf, kn_ref, vn_ref, acc_ref, carry_ref = refs[2 * n_step + 1:]
    p = pl.program_id(1)
    page = kn_ref.shape[0]
    rows = SB_SAMPLE_ROWS
    upper = _upper(page)

    def heads(fn):
        for h in range(N_HEADS):
            fn(h, slice(h * D_HEAD, (h + 1) * D_HEAD))

    @pl.when(p == 0)
    def _():
        q_bf[...] = jnp.zeros_like(q_bf)
        q_bf[0:n_new, :] = q_ref[...] * (D_HEAD ** -0.5)
        kn_ref[...] = jnp.zeros_like(kn_ref)
        vn_ref[...] = jnp.zeros_like(vn_ref)
        kn_ref[0:n_new, :] = k_ref[...]
        vn_ref[0:n_new, :] = v_ref[...]
        mask = _iota2((rows, page), 1) < _iota2((rows, page), 0)

        def new_tokens(h, cols):
            out, carry = _sb_block(q_bf[:, cols].astype(BF16), kn_ref[:, cols].astype(BF16),
                                   vn_ref[:, cols].astype(BF16), bias_ref[h], upper, mask,
                                   jnp.zeros((rows, page), F32))
            acc_ref[:, cols] = out
            carry_ref[h * rows:(h + 1) * rows, :] = carry

        heads(new_tokens)

    def head_rows(ref, h):
        return ref[pl.ds(h, page, stride=N_HEADS), :].astype(BF16)

    z = jnp.concatenate(
        [_dot_nt(q_bf[:, h * D_HEAD:(h + 1) * D_HEAD].astype(BF16),
                 jnp.concatenate([head_rows(kp_ref, h) for kp_ref in kp_refs], axis=0)) + bias_ref[h]
         for h in range(N_HEADS)], axis=0)
    sp = _softplus_neg_abs(z)
    log_beta = jnp.minimum(z, 0.0) - sp
    log_keep = jnp.minimum(-z, 0.0) - sp
    hi, lo = _split_bf16(log_keep)
    stacked = jnp.concatenate([hi, lo], axis=0)
    n_rows = N_HEADS * rows
    carry = carry_ref[...]
    ws = []
    for j in range(n_step):
        seg = slice(j * page, (j + 1) * page)
        after2 = _dot(stacked[:, seg], upper)
        ws.append(jnp.exp(log_beta[:, seg] + (after2[:n_rows] + after2[n_rows:]) + carry).astype(BF16))
        carry = carry + jnp.sum(log_keep[:, seg], axis=1, keepdims=True)
    carry_ref[...] = carry
    w = jnp.concatenate(ws, axis=1)
    acc_ref[...] += jnp.concatenate(
        [_dot(w[h * rows:(h + 1) * rows], jnp.concatenate([head_rows(vp_ref, h) for vp_ref in vp_refs], axis=0))
         for h in range(N_HEADS)], axis=1)

    @pl.when(p == pl.num_programs(1) - 1)
    def _():
        o_ref[...] = acc_ref[0:n_new, :]


def _sb_sample(proj, bias, cache_k, cache_v, page_table, layer, n_seq, n_new, row_block, out_buf):
    t = proj.shape[0]
    n_pages = page_table.shape[1]
    page = cache_k.shape[2]
    assert n_new <= SB_SAMPLE_ROWS <= page
    n_step = SB_PAGES_PER_STEP if n_pages % SB_PAGES_PER_STEP == 0 else 1
    k_rows = cache_k.reshape(cache_k.shape[:2] + (page * N_HEADS, D_HEAD))
    v_rows = cache_v.reshape(cache_v.shape[:2] + (page * N_HEADS, D_HEAD))
    tok = lambda col: pl.BlockSpec((n_new, D_MODEL), lambda b, p, pt: (b + row_block, col // N_HEADS))

    def paged(j):
        return pl.BlockSpec((None, None, page * N_HEADS, D_HEAD),
                            lambda b, p, pt: (pt[b, n_pages - 1 - (p * n_step + j)], layer, 0, 0))

    pages = [paged(j) for j in range(n_step)]
    return pl.pallas_call(
        functools.partial(_sb_sample_kernel, n_new, n_step),
        out_shape=jax.ShapeDtypeStruct((t, D_MODEL), F32),
        grid_spec=pltpu.PrefetchScalarGridSpec(
            num_scalar_prefetch=1,
            grid=(n_seq, n_pages // n_step),
            in_specs=[pl.BlockSpec(memory_space=pltpu.SMEM), tok(COL_C_Q), tok(COL_C_K), tok(COL_C_V)]
            + pages + pages + [pl.BlockSpec(memory_space=pl.ANY)],
            out_specs=pl.BlockSpec((n_new, D_MODEL), lambda b, p, pt: (b + row_block, 0)),
            scratch_shapes=[pltpu.VMEM((SB_SAMPLE_ROWS, D_MODEL), F32),
                            pltpu.VMEM((page, D_MODEL), F32), pltpu.VMEM((page, D_MODEL), F32),
                            pltpu.VMEM((SB_SAMPLE_ROWS, D_MODEL), F32),
                            pltpu.VMEM((N_HEADS * SB_SAMPLE_ROWS, page), F32)],
        ),
        input_output_aliases={5 + 2 * n_step: 0},
        compiler_params=_cparams("parallel", "arbitrary"),
        name="sb_sample",
    )(page_table, bias, proj, proj, proj, *([k_rows] * n_step), *([v_rows] * n_step), out_buf)


def _token_tile(t):
    for tm in (1280, 1024, 512, 256, 128, 64, 32, 16, 8):
        if t % tm == 0:
            return tm
    raise ValueError(f"token count {t} is not a multiple of 8")


def _gate_rows(gates, n_seq, seq_len):
    g = gates[:, :2 * N_HEADS].reshape(n_seq, seq_len, 2, N_HEADS)
    return jnp.transpose(g, (0, 3, 2, 1))


def kernel(x_prompt, x_sample, state_hgrn, state_mlstm_C, state_mlstm_n, state_mlstm_m, cache_k, cache_v, page_table, meta_tokens, g_ffn1, w_ffn1_up, w_ffn1_down, g_mix, w_in, lb_param, b_ig, b_fg, b_sb, g_norm_a, g_norm_b, w_out, g_ffn2, w_ffn2_up, w_ffn2_down, g_final):
    depth = w_in.shape[0]
    n_p, s_p, _ = x_prompt.shape
    n_s, l_s, _ = x_sample.shape
    l_p = s_p + N_META
    t_p, t_s = n_p * l_p, n_s * l_s
    t = t_p + t_s
    tm = _token_tile(t)
    rb_s = t_p // l_s
    assert t_p % l_s == 0

    meta = jnp.broadcast_to(meta_tokens.astype(x_prompt.dtype)[None], (n_p, N_META, D_MODEL))
    h = jnp.concatenate([jnp.concatenate([meta, x_prompt], axis=1).reshape(t_p, D_MODEL),
                         x_sample.reshape(t_s, D_MODEL)], axis=0)

    lb = jnp.cumsum(jax.nn.softmax(lb_param.astype(F32), axis=0), axis=0)
    lb = lb - lb[0]
    lb_pack = jnp.stack([lb, jnp.log(lb), jnp.log1p(-lb)], axis=1)

    w_main = jnp.concatenate([w_in[:, :, :GATE_OFFSET], w_in[:, :, GATE_OFFSET + 2 * N_HEADS:]], axis=2).astype(BF16)
    w_gate = jnp.pad(w_in[:, :, GATE_OFFSET:GATE_OFFSET + 2 * N_HEADS],
                     ((0, 0), (0, 0), (0, LANES - 2 * N_HEADS))).astype(BF16)
    up1, down1 = w_ffn1_up.astype(BF16), w_ffn1_down.astype(BF16)
    up2, down2 = w_ffn2_up.astype(BF16), w_ffn2_down.astype(BF16)
    w_o = w_out.astype(BF16)
    gate_bias = jnp.concatenate([b_ig, b_fg], axis=1).astype(F32)
    m0 = jnp.broadcast_to(state_mlstm_m[..., None, None], state_mlstm_m.shape + (1, LANES))
    n0 = state_mlstm_n[:, :, :, None, :]

    outs = {k: [] for k in ("pS", "pC", "pn", "pm", "pk", "pv", "sS", "sC", "sn", "sm", "sk", "sv")}
    for l in range(depth):
        row = lambda a: a[l][None].astype(F32)
        h = _ffn(h, row(g_ffn1), up1[l], down1[l], tm)
        proj, gates = _inproj(h, row(g_mix), w_main[l], w_gate[l], tm)

        gn_a, gn_b = row(g_norm_a), row(g_norm_b)
        h_a, s_p_l = _hgrn(proj, lb_pack[l], gn_a, n_p, l_p, N_META, 0)
        h_a, s_s_l = _hgrn(proj, lb_pack[l], gn_a, n_s, l_s, 0, rb_s, state=state_hgrn[l], out_buf=h_a)

        gr_p = _gate_rows(gates[:t_p], n_p, l_p)
        gr_s = _gate_rows(gates[t_p:], n_s, l_s)
        h_b, c_p, nn_p, m_p = _mlstm(proj, gates, gr_p[..., :N_META], gr_p[..., N_META:], gate_bias[l], gn_b,
                                     n_p, l_p, N_META, 0)
        h_b, c_s, nn_s, m_s = _mlstm(proj, gates, None, gr_s, gate_bias[l], gn_b, n_s, l_s, 0, rb_s,
                                     state=(state_mlstm_C[l], n0[l], m0[l]), out_buf=h_b)

        sb_bias = b_sb[l].astype(F32)
        h_c = _sb_prompt(proj, sb_bias, n_p, l_p, N_META)
        h_c = _sb_sample(proj, sb_bias, cache_k, cache_v, page_table, l, n_s, l_s, rb_s, h_c)

        h = _merge(h, proj, h_a, h_b, h_c, w_o[l], tm)
        h = _ffn(h, row(g_ffn2), up2[l], down2[l], tm)

        k_c = proj[:, COL_C_K * LANES:(COL_C_K + N_HEADS) * LANES]
        v_c = proj[:, COL_C_V * LANES:(COL_C_V + N_HEADS) * LANES]
        outs["pS"].append(s_p_l)
        outs["pC"].append(c_p)
        outs["pn"].append(nn_p[:, :, 0])
        outs["pm"].append(m_p[:, :, 0, 0])
        outs["pk"].append(k_c[:t_p].reshape(n_p, l_p, N_HEADS, D_HEAD))
        outs["pv"].append(v_c[:t_p].reshape(n_p, l_p, N_HEADS, D_HEAD))
        outs["sS"].append(s_s_l)
        outs["sC"].append(c_s)
        outs["sn"].append(nn_s[:, :, 0])
        outs["sm"].append(m_s[:, :, 0, 0])
        outs["sk"].append(k_c[t_p:].reshape(n_s, l_s, N_HEADS, D_HEAD))
        outs["sv"].append(v_c[t_p:].reshape(n_s, l_s, N_HEADS, D_HEAD))

    g_fin = g_final[None].astype(F32)
    y_prompt = _final_norm(h, g_fin, n_p, l_p, N_META, 0).reshape(n_p, s_p, D_MODEL)
    y_sample = _final_norm(h, g_fin, n_s, l_s, 0, rb_s).reshape(n_s, l_s, D_MODEL)
    st = lambda k, axis=0: jnp.stack(outs[k], axis=axis)
    return (y_prompt, y_sample, st("pS"), st("pC"), st("pn"), st("pm"), st("pk", 1), st("pv", 1),
            st("sS"), st("sC"), st("sn"), st("sm"), st("sk", 1), st("sv", 1))
```

```python
import functools

import numpy as np
import jax
import jax.numpy as jnp
from jax import lax
from jax.experimental import pallas as pl
from jax.experimental.pallas import tpu as pltpu

F32 = jnp.float32
BF16 = jnp.bfloat16

D_MODEL = 1024
N_HEADS = 8
D_HEAD = 128
D_FF = 2816
N_META = 16
EPS = 1e-6

LANES = 128
VMEM_LIMIT_BYTES = 56 * 1024 * 1024

COL_A_Q, COL_A_F, COL_A_I, COL_A_G = 0, 8, 16, 24
COL_B_Q, COL_B_K, COL_B_V, COL_B_O = 32, 40, 48, 56
COL_C_Q, COL_C_K, COL_C_V = 64, 72, 80
COL_G_A, COL_G_B, COL_G_C = 88, 96, 104
MAIN_WIDTH = 112 * LANES
GATE_OFFSET = 8 * D_MODEL

HEADS_PER_STEP = 4
HGRN_CHUNK = 128
HGRN_SUB = 16
MLSTM_CHUNK = 128
SB_BLOCK = 256
SB_PAGES_PER_STEP = 8

FF_TILE = 256
IN_TILE = 1024
MERGE_TILE = 256
SB_SAMPLE_ROWS = 16


def _cparams(*sem):
    return pltpu.CompilerParams(dimension_semantics=sem, vmem_limit_bytes=VMEM_LIMIT_BYTES)


def _dot(a, b):
    return jnp.dot(a, b, preferred_element_type=F32)


def _dot_nt(a, b):
    return lax.dot_general(a, b, (((1,), (1,)), ((), ())), preferred_element_type=F32)


def _dot_tn(a, b):
    return lax.dot_general(a, b, (((0,), (0,)), ((), ())), preferred_element_type=F32)


def _rms(x, g):
    return x * lax.rsqrt(jnp.mean(x * x, axis=-1, keepdims=True) + EPS) * g


def _softplus_neg_abs(z):
    return jnp.log(1.0 + jnp.exp(-jnp.abs(z)))


def _log_sigmoid(z):
    return jnp.minimum(z, 0.0) - _softplus_neg_abs(z)


def _split_bf16(x):
    hi = x.astype(BF16)
    lo = (x - hi.astype(F32)).astype(BF16)
    return hi, lo


def _iota2(shape, dim):
    return lax.broadcasted_iota(jnp.int32, shape, dim)


def _ffn_kernel(h_ref, g_ref, wg_ref, wu_ref, wd_ref, o_ref, xn_ref, acc_ref):
    f = pl.program_id(1)

    @pl.when(f == 0)
    def _():
        xn_ref[...] = _rms(h_ref[...], g_ref[...]).astype(BF16)
        acc_ref[...] = jnp.zeros_like(acc_ref)

    xn = xn_ref[...]
    gate = _dot(xn, wg_ref[...])
    up = _dot(xn, wu_ref[...])
    act = (gate * jax.nn.sigmoid(gate) * up).astype(BF16)
    acc_ref[...] += _dot(act, wd_ref[...])

    @pl.when(f == pl.num_programs(1) - 1)
    def _():
        o_ref[...] = h_ref[...] + 0.5 * acc_ref[...]


def _ffn(h, g, w_up, w_down, tm):
    t = h.shape[0]
    nf = D_FF // FF_TILE
    return pl.pallas_call(
        _ffn_kernel,
        out_shape=jax.ShapeDtypeStruct((t, D_MODEL), F32),
        grid=(t // tm, nf),
        in_specs=[
            pl.BlockSpec((tm, D_MODEL), lambda i, f: (i, 0)),
            pl.BlockSpec((1, D_MODEL), lambda i, f: (0, 0)),
            pl.BlockSpec((D_MODEL, FF_TILE), lambda i, f: (0, f)),
            pl.BlockSpec((D_MODEL, FF_TILE), lambda i, f: (0, f + D_FF // FF_TILE)),
            pl.BlockSpec((FF_TILE, D_MODEL), lambda i, f: (f, 0)),
        ],
        out_specs=pl.BlockSpec((tm, D_MODEL), lambda i, f: (i, 0)),
        scratch_shapes=[pltpu.VMEM((tm, D_MODEL), BF16), pltpu.VMEM((tm, D_MODEL), F32)],
        compiler_params=_cparams("parallel", "arbitrary"),
        name="ffn",
    )(h, g, w_up, w_up, w_down)


def _inproj_kernel(h_ref, g_ref, w_ref, wg_ref, o_ref, og_ref, xn_ref):
    @pl.when(pl.program_id(1) == 0)
    def _():
        xn = _rms(h_ref[...], g_ref[...]).astype(BF16)
        xn_ref[...] = xn
        og_ref[...] = _dot(xn, wg_ref[...])

    o_ref[...] = _dot(xn_ref[...], w_ref[...])


def _inproj(h, g, w_main, w_gate, tm):
    t = h.shape[0]
    return pl.pallas_call(
        _inproj_kernel,
        out_shape=(jax.ShapeDtypeStruct((t, MAIN_WIDTH), F32),
                   jax.ShapeDtypeStruct((t, LANES), F32)),
        grid=(t // tm, MAIN_WIDTH // IN_TILE),
        in_specs=[
            pl.BlockSpec((tm, D_MODEL), lambda i, j: (i, 0)),
            pl.BlockSpec((1, D_MODEL), lambda i, j: (0, 0)),
            pl.BlockSpec((D_MODEL, IN_TILE), lambda i, j: (0, j)),
            pl.BlockSpec((D_MODEL, LANES), lambda i, j: (0, 0)),
        ],
        out_specs=(pl.BlockSpec((tm, IN_TILE), lambda i, j: (i, j)),
                   pl.BlockSpec((tm, LANES), lambda i, j: (i, 0))),
        scratch_shapes=[pltpu.VMEM((tm, D_MODEL), BF16)],
        compiler_params=_cparams("parallel", "arbitrary"),
        name="inproj",
    )(h, g, w_main, w_gate)


def _merge_kernel(h_ref, ga_ref, gb_ref, gc_ref, ha_ref, hb_ref, hc_ref, w_ref, o_ref):
    merged = (jax.nn.sigmoid(ga_ref[...]) * ha_ref[...]
              + jax.nn.sigmoid(gb_ref[...]) * hb_ref[...]
              + jax.nn.sigmoid(gc_ref[...]) * hc_ref[...])
    o_ref[...] = h_ref[...] + _dot(merged.astype(BF16), w_ref[...])


def _merge(h, proj, h_a, h_b, h_c, w_out, tm):
    t = h.shape[0]
    tm = min(tm, MERGE_TILE)
    row = lambda i: (i, 0)
    col = lambda c: (lambda i: (i, c // N_HEADS))
    return pl.pallas_call(
        _merge_kernel,
        out_shape=jax.ShapeDtypeStruct((t, D_MODEL), F32),
        grid=(t // tm,),
        in_specs=[
            pl.BlockSpec((tm, D_MODEL), row),
            pl.BlockSpec((tm, D_MODEL), col(COL_G_A)),
            pl.BlockSpec((tm, D_MODEL), col(COL_G_B)),
            pl.BlockSpec((tm, D_MODEL), col(COL_G_C)),
            pl.BlockSpec((tm, D_MODEL), row),
            pl.BlockSpec((tm, D_MODEL), row),
            pl.BlockSpec((tm, D_MODEL), row),
            pl.BlockSpec((D_MODEL, D_MODEL), lambda i: (0, 0)),
        ],
        out_specs=pl.BlockSpec((tm, D_MODEL), row),
        compiler_params=_cparams("parallel"),
        name="merge_outproj",
    )(h, proj, proj, proj, h_a, h_b, h_c, w_out)


def _final_norm_kernel(lead, h_ref, g_ref, o_ref):
    o_ref[...] = _rms(h_ref[pl.ds(lead, o_ref.shape[0]), :], g_ref[...])


def _final_norm(h, g, n_seq, seq_len, lead, row_block):
    out_len = seq_len - lead
    return pl.pallas_call(
        functools.partial(_final_norm_kernel, lead),
        out_shape=jax.ShapeDtypeStruct((n_seq * out_len, D_MODEL), F32),
        grid=(n_seq,),
        in_specs=[pl.BlockSpec((seq_len, D_MODEL), lambda b: (b + row_block, 0)),
                  pl.BlockSpec((1, D_MODEL), lambda b: (0, 0))],
        out_specs=pl.BlockSpec((out_len, D_MODEL), lambda b: (b, 0)),
        compiler_params=_cparams("parallel"),
        name="final_norm",
    )(h, g)


def _seq_spec(seq_len, row_block, col, hp=1):
    assert col % hp == 0
    return pl.BlockSpec((seq_len, hp * LANES), lambda b, h: (b + row_block, col // hp + h))


def _head_spec(seq_len, row_block, hp=1):
    return pl.BlockSpec((seq_len, hp * LANES), lambda b, h: (b + row_block, h))


def _state_spec(*tail):
    return pl.BlockSpec((None, HEADS_PER_STEP) + tail, lambda b, h: (b, h) + (0,) * len(tail))


def _head_cols(hh):
    return slice(hh * LANES, (hh + 1) * LANES)


def _chunk_plan(seq_len, lead, chunk):
    main = seq_len - lead
    size = min(chunk, main)
    assert main % size == 0
    return size, main // size


def _for_chunks(n, body, init):
    if n == 1:
        return body(0, init)
    return lax.fori_loop(0, n, body, init)


def _aligned(x, m):
    return x if isinstance(x, int) else pl.multiple_of(x, m)


def _head_norm(o, g):
    return o * lax.rsqrt(jnp.mean(o * o, axis=-1, keepdims=True) + EPS) * g


def _hgrn_levels(c):
    sizes, size = [], 2 * HGRN_SUB
    while size <= c:
        sizes.append(size)
        size *= 2
    assert not sizes or sizes[-1] == c
    return sizes


def _hgrn_cum_matrix(c):
    t = np.arange(c)[:, None]
    s = np.arange(c)[None, :]
    mats = [(s <= t).astype(np.float32)]
    if c > HGRN_SUB:
        mats.append(((s <= t) & (t // HGRN_SUB == s // HGRN_SUB)).astype(np.float32))
        for size in _hgrn_levels(c):
            mid = (t // size) * size + size // 2 - 1
            mats.append((s <= t).astype(np.float32) - (s <= mid).astype(np.float32))
    return jnp.asarray(np.concatenate(mats, axis=0), BF16)


def _hgrn_masks(c):
    rows, cols = _iota2((c, c), 0), _iota2((c, c), 1)
    diag = cols <= rows
    if c <= HGRN_SUB:
        return diag, []
    shift = HGRN_SUB.bit_length() - 1
    diag = diag & ((rows >> shift) == (cols >> shift))
    levels = []
    for size in _hgrn_levels(c):
        shift = size.bit_length() - 1
        half = size // 2
        levels.append(((rows >> shift) == (cols >> shift)) & ((rows & half) != 0) & ((cols & half) == 0))
    return diag, levels


def _hgrn_chunk(st, q, af, iv, lb, cum_mat, masks):
    c = q.shape[0]
    lbv, log_lb, log_1m_lb = lb[0:1], lb[1:2], lb[2:3]
    x = log_1m_lb + _log_sigmoid(af)
    logf = jnp.maximum(log_lb, x) + _softplus_neg_abs(log_lb - x)
    k = (1.0 - lbv) * jax.nn.sigmoid(-af)

    hi, lo = _split_bf16(logf)
    rel2 = _dot(cum_mat, jnp.concatenate([hi, lo], axis=1))
    rel = rel2[:, :LANES] + rel2[:, LANES:]
    cum = rel[:c]

    inter = _dot_nt((q * jnp.exp(cum)).astype(BF16), st.astype(BF16))

    mask_diag, mask_levels = masks
    d = rel[c:2 * c] if mask_levels else cum
    scores = jnp.where(mask_diag, _dot_nt((q * jnp.exp(d)).astype(BF16), (k * jnp.exp(-d)).astype(BF16)), 0.0)
    for lev, mask in enumerate(mask_levels):
        d = rel[(2 + lev) * c:(3 + lev) * c]
        qh = q * jnp.exp(jnp.minimum(d, 0.0))
        kh = k * jnp.exp(jnp.minimum(-d, 0.0))
        scores += jnp.where(mask, _dot_nt(qh.astype(BF16), kh.astype(BF16)), 0.0)
    iv_bf = iv.astype(BF16)
    o = inter + _dot(scores.astype(BF16), iv_bf)

    last = cum[c - 1:c]
    kd = k * jnp.exp(last - cum)
    st_new = st * jnp.exp(last) + _dot_tn(iv_bf, kd.astype(BF16))
    return st_new, o


def _hgrn_kernel(seq_len, lead, has_state, *refs):
    heads = range(HEADS_PER_STEP)
    if has_state:
        q_ref, f_ref, i_ref, g_ref, lb_ref, gn_ref, cm_ref, s0_ref, _, o_ref, s_ref = refs
        sts = tuple(s0_ref[hh].T for hh in heads)
    else:
        q_ref, f_ref, i_ref, g_ref, lb_ref, gn_ref, cm_ref, o_ref, s_ref = refs
        sts = tuple(jnp.zeros((D_HEAD, D_HEAD), F32) for _ in heads)
    lb = lb_ref[...]
    gn = gn_ref[...]
    size, n_main = _chunk_plan(seq_len, lead, HGRN_CHUNK)

    def run(sts, rows, cum_mat, masks):
        new_sts, outs = [], []
        for hh in heads:
            cols = _head_cols(hh)
            st, o = _hgrn_chunk(sts[hh], q_ref[rows, cols], f_ref[rows, cols], i_ref[rows, cols],
                                lb[:, cols], cum_mat, masks)
            gate = g_ref[rows, cols]
            new_sts.append(st)
            outs.append(_head_norm(o, gn[:, cols]) * (gate * jax.nn.sigmoid(gate)))
        o_ref[rows, :] = jnp.concatenate(outs, axis=1)
        return tuple(new_sts)

    if lead > 0:
        assert lead <= HGRN_SUB
        tril = (_iota2((lead, lead), 0) >= _iota2((lead, lead), 1)).astype(BF16)
        sts = run(sts, pl.ds(0, lead), tril, _hgrn_masks(lead))
    cum_mat, masks = cm_ref[...], _hgrn_masks(size)
    sts = _for_chunks(
        n_main, lambda ci, sts: run(sts, pl.ds(_aligned(lead + ci * size, 8), size), cum_mat, masks), sts)
    for hh in heads:
        s_ref[hh] = sts[hh].T


def _hgrn(proj, lb_pack, g_norm, n_seq, seq_len, lead, row_block, state=None, out_buf=None):
    t = proj.shape[0]
    has_state = state is not None
    cum_mat = _hgrn_cum_matrix(_chunk_plan(seq_len, lead, HGRN_CHUNK)[0])
    hp = HEADS_PER_STEP
    in_specs = [
        _seq_spec(seq_len, row_block, COL_A_Q, hp),
        _seq_spec(seq_len, row_block, COL_A_F, hp),
        _seq_spec(seq_len, row_block, COL_A_I, hp),
        _seq_spec(seq_len, row_block, COL_A_G, hp),
        pl.BlockSpec((3, hp * LANES), lambda b, h: (0, h)),
        pl.BlockSpec((1, hp * LANES), lambda b, h: (0, h)),
        pl.BlockSpec(cum_mat.shape, lambda b, h: (0, 0)),
    ]
    args = [proj, proj, proj, proj, lb_pack, g_norm, cum_mat]
    aliases = {}
    if has_state:
        in_specs += [_state_spec(D_HEAD, D_HEAD), pl.BlockSpec(memory_space=pl.ANY)]
        args += [state, out_buf]
        aliases = {len(args) - 1: 0}
    return pl.pallas_call(
        functools.partial(_hgrn_kernel, seq_len, lead, has_state),
        out_shape=(jax.ShapeDtypeStruct((t, D_MODEL), F32),
                   jax.ShapeDtypeStruct((n_seq, N_HEADS, D_HEAD, D_HEAD), F32)),
        grid=(n_seq, N_HEADS // hp),
        in_specs=in_specs,
        out_specs=(_head_spec(seq_len, row_block, hp), _state_spec(D_HEAD, D_HEAD)),
        input_output_aliases=aliases,
        compiler_params=_cparams("parallel", "parallel"),
        name="hgrn2",
    )(*args)


def _mlstm_chunk(cst, n, m, q, ks, v, ig_col, lf_col, ig_row, lf_row):
    c = q.shape[0]
    rows, cols = _iota2((c, c), 0), _iota2((c, c), 1)
    causal = rows >= cols
    b_col = jnp.sum(jnp.where(causal, lf_row, 0.0), axis=1, keepdims=True)
    b_row = jnp.sum(jnp.where(rows <= cols, lf_col, 0.0), axis=0, keepdims=True)
    log_d = jnp.where(causal, b_col - b_row + ig_row, -jnp.inf)
    m_t = jnp.maximum(b_col + m, jnp.max(log_d, axis=1, keepdims=True))
    w = jnp.exp(log_d - m_t)
    inter = jnp.exp(b_col + m - m_t)
    q_bf, k_bf, v_bf = q.astype(BF16), ks.astype(BF16), v.astype(BF16)
    ws = w * _dot_nt(q_bf, k_bf)
    num = inter * _dot(q_bf, cst.astype(BF16)) + _dot(ws.astype(BF16), v_bf)
    den = inter * jnp.sum(q * n, axis=1, keepdims=True) + jnp.sum(ws, axis=1, keepdims=True)
    h = num / jnp.maximum(jnp.abs(den), jnp.exp(-m_t))
    m_new = m_t[c - 1:c]
    b_last = b_col[c - 1:c]
    wk = jnp.exp(b_last - b_col + ig_col - m_new)
    carry = jnp.exp(b_last + m - m_new)
    kd = ks * wk
    c_new = carry * cst + _dot_tn(kd.astype(BF16), v_bf)
    n_new = carry * n + jnp.sum(kd, axis=0, keepdims=True)
    return c_new, n_new, m_new, h


def _mlstm_kernel(seq_len, lead, has_state, *refs):
    bias_ref, q_ref, k_ref, v_ref, bo_ref, gc_ref, gn_ref = refs[:7]
    refs = refs[7:]
    if lead > 0:
        grl_ref, refs = refs[0], refs[1:]
    grm_ref, refs = refs[0], refs[1:]
    if has_state:
        c0_ref, n0_ref, m0_ref, _, o_ref, c_ref, n_ref, m_ref = refs
        states = tuple((c0_ref[hh], n0_ref[hh], m0_ref[hh][:, 0:1]) for hh in range(HEADS_PER_STEP))
    else:
        o_ref, c_ref, n_ref, m_ref = refs
        states = tuple((jnp.zeros((D_HEAD, D_HEAD), F32), jnp.zeros((1, D_HEAD), F32), jnp.zeros((1, 1), F32))
                       for _ in range(HEADS_PER_STEP))
    gn = gn_ref[...]
    size, n_main = _chunk_plan(seq_len, lead, MLSTM_CHUNK)

    def run(states, rows, gate_rows):
        gates = gc_ref[rows, :]
        lane = _iota2(gates.shape, 1)
        new_states, outs = [], []
        for hh in range(HEADS_PER_STEP):
            cols = _head_cols(hh)
            head = pl.program_id(1) * HEADS_PER_STEP + hh
            b_ig = bias_ref[head]
            b_fg = bias_ref[N_HEADS + head]
            ig_col = jnp.sum(jnp.where(lane == head, gates, 0.0), axis=1, keepdims=True) + b_ig
            fg_col = jnp.sum(jnp.where(lane == head + N_HEADS, gates, 0.0), axis=1, keepdims=True) + b_fg
            ig_row = gate_rows[hh][0:1] + b_ig
            fg_row = gate_rows[hh][1:2] + b_fg
            c_new, n_new, m_new, h = _mlstm_chunk(
                *states[hh], q_ref[rows, cols], k_ref[rows, cols] * (D_HEAD ** -0.5), v_ref[rows, cols],
                ig_col, _log_sigmoid(fg_col), ig_row, _log_sigmoid(fg_row))
            new_states.append((c_new, n_new, m_new))
            outs.append(_head_norm(h, gn[:, cols]) * jax.nn.sigmoid(bo_ref[rows, cols]))
        o_ref[rows, :] = jnp.concatenate(outs, axis=1)
        return tuple(new_states)

    if lead > 0:
        states = run(states, pl.ds(0, lead), [grl_ref[hh] for hh in range(HEADS_PER_STEP)])

    def body(ci, states):
        off = _aligned(ci * size, size)
        return run(states, pl.ds(_aligned(lead + ci * size, 8), size),
                   [grm_ref[hh, :, pl.ds(off, size)] for hh in range(HEADS_PER_STEP)])

    states = _for_chunks(n_main, body, states)
    for hh in range(HEADS_PER_STEP):
        c_ref[hh] = states[hh][0]
        n_ref[hh] = states[hh][1]
        m_ref[hh] = jnp.broadcast_to(states[hh][2], (1, LANES))


def _mlstm(proj, gates, gate_rows_lead, gate_rows_main, bias, g_norm, n_seq, seq_len, lead, row_block,
           state=None, out_buf=None):
    t = proj.shape[0]
    has_state = state is not None
    hp = HEADS_PER_STEP
    in_specs = [
        pl.BlockSpec(memory_space=pltpu.SMEM),
        _seq_spec(seq_len, row_block, COL_B_Q, hp),
        _seq_spec(seq_len, row_block, COL_B_K, hp),
        _seq_spec(seq_len, row_block, COL_B_V, hp),
        _seq_spec(seq_len, row_block, COL_B_O, hp),
        pl.BlockSpec((seq_len, LANES), lambda b, h: (b + row_block, 0)),
        pl.BlockSpec((1, hp * LANES), lambda b, h: (0, h)),
    ]
    args = [bias, proj, proj, proj, proj, gates, g_norm]
    if lead > 0:
        in_specs.append(_state_spec(2, lead))
        args.append(gate_rows_lead)
    in_specs.append(_state_spec(2, seq_len - lead))
    args.append(gate_rows_main)
    aliases = {}
    if has_state:
        in_specs += [_state_spec(D_HEAD, D_HEAD), _state_spec(1, D_HEAD), _state_spec(1, LANES),
                     pl.BlockSpec(memory_space=pl.ANY)]
        args += [state[0], state[1], state[2], out_buf]
        aliases = {len(args) - 1: 0}
    return pl.pallas_call(
        functools.partial(_mlstm_kernel, seq_len, lead, has_state),
        out_shape=(jax.ShapeDtypeStruct((t, D_MODEL), F32),
                   jax.ShapeDtypeStruct((n_seq, N_HEADS, D_HEAD, D_HEAD), F32),
                   jax.ShapeDtypeStruct((n_seq, N_HEADS, 1, D_HEAD), F32),
                   jax.ShapeDtypeStruct((n_seq, N_HEADS, 1, LANES), F32)),
        grid=(n_seq, N_HEADS // hp),
        in_specs=in_specs,
        out_specs=(_head_spec(seq_len, row_block, hp), _state_spec(D_HEAD, D_HEAD),
                   _state_spec(1, D_HEAD), _state_spec(1, LANES)),
        input_output_aliases=aliases,
        compiler_params=_cparams("parallel", "parallel"),
        name="mlstm",
    )(*args)


def _sb_block(qb, kb, vb, bias, upper, mask, carry):
    w, carry = _sb_weights(_dot_nt(qb, kb) + bias, upper, mask, carry)
    return _dot(w.astype(BF16), vb), carry


def _sb_weights(z, upper, mask, carry):
    tq = z.shape[0]
    sp = _softplus_neg_abs(z)
    log_beta = jnp.minimum(z, 0.0) - sp
    log_keep = jnp.minimum(-z, 0.0) - sp
    if mask is not None:
        log_keep = jnp.where(mask, log_keep, 0.0)
    hi, lo = _split_bf16(log_keep)
    after2 = _dot(jnp.concatenate([hi, lo], axis=0), upper)
    w = jnp.exp(log_beta + (after2[:tq] + after2[tq:]) + carry)
    if mask is not None:
        w = jnp.where(mask, w, 0.0)
    return w, carry + jnp.sum(log_keep, axis=1, keepdims=True)


def _upper(n):
    return (_iota2((n, n), 0) > _iota2((n, n), 1)).astype(BF16)


def _sb_prompt_kernel(seq_len, lead, bias_ref, q_ref, k_ref, v_ref, o_ref, q_bf, k_bf, v_bf, acc_ref):
    bias = bias_ref[pl.program_id(1)]
    q_bf[...] = (q_ref[...] * (D_HEAD ** -0.5)).astype(BF16)
    k_bf[...] = k_ref[...].astype(BF16)
    v_bf[...] = v_ref[...].astype(BF16)
    blk = SB_BLOCK
    n_pairs = (seq_len - lead) // (2 * blk)
    upper = _upper(blk)
    strict = _iota2((blk, blk), 1) < _iota2((blk, blk), 0)
    zero_carry = jnp.zeros((blk, 1), F32)

    if lead > 0:
        lead_rows = pl.ds(0, lead)
        lead_mask = _iota2((lead, lead), 1) < _iota2((lead, lead), 0)
        out, _ = _sb_block(q_bf[lead_rows, :], k_bf[lead_rows, :], v_bf[lead_rows, :], bias,
                           _upper(lead), lead_mask, jnp.zeros((lead, 1), F32))
        o_ref[lead_rows, :] = out

    def rows_of(j):
        return pl.ds(_aligned(lead + j * blk, 16), blk)

    def attend(qb, rows, mask, carry):
        return _sb_block(qb, k_bf[rows, :], v_bf[rows, :], bias, upper, mask, carry)

    def query_pair(i, _):
        r_lo, r_hi = rows_of(2 * i), rows_of(2 * i + 1)
        q_lo, q_hi = q_bf[r_lo, :], q_bf[r_hi, :]
        acc_lo, c_lo = attend(q_lo, r_lo, strict, zero_carry)
        acc_hi, c_hi = attend(q_hi, r_hi, strict, zero_carry)
        out, c_hi = attend(q_hi, r_lo, None, c_hi)
        acc_ref[0] = acc_lo
        acc_ref[1] = acc_hi + out

        def key_pair(jj, carries):
            c_lo, c_hi = carries
            newer, older = rows_of(2 * (i - 1 - jj) + 1), rows_of(2 * (i - 1 - jj))
            o_lo, c_lo = attend(q_lo, newer, None, c_lo)
            o_hi, c_hi = attend(q_hi, newer, None, c_hi)
            p_lo, c_lo = attend(q_lo, older, None, c_lo)
            p_hi, c_hi = attend(q_hi, older, None, c_hi)
            acc_ref[0] += o_lo + p_lo
            acc_ref[1] += o_hi + p_hi
            return c_lo, c_hi

        c_lo, c_hi = lax.fori_loop(0, i, key_pair, (c_lo, c_hi))
        acc_lo, acc_hi = acc_ref[0], acc_ref[1]
        if lead > 0:
            k_lead, v_lead, up_lead = k_bf[lead_rows, :], v_bf[lead_rows, :], _upper(lead)
            acc_lo += _sb_block(q_lo, k_lead, v_lead, bias, up_lead, None, c_lo)[0]
            acc_hi += _sb_block(q_hi, k_lead, v_lead, bias, up_lead, None, c_hi)[0]
        o_ref[r_lo, :] = acc_lo
        o_ref[r_hi, :] = acc_hi
        return 0

    lax.fori_loop(0, n_pairs, query_pair, 0)


def _sb_prompt(proj, bias, n_seq, seq_len, lead):
    t = proj.shape[0]
    assert (seq_len - lead) % (2 * SB_BLOCK) == 0
    return pl.pallas_call(
        functools.partial(_sb_prompt_kernel, seq_len, lead),
        out_shape=jax.ShapeDtypeStruct((t, D_MODEL), F32),
        grid=(n_seq, N_HEADS),
        in_specs=[
            pl.BlockSpec(memory_space=pltpu.SMEM),
            _seq_spec(seq_len, 0, COL_C_Q),
            _seq_spec(seq_len, 0, COL_C_K),
            _seq_spec(seq_len, 0, COL_C_V),
        ],
        out_specs=_head_spec(seq_len, 0),
        scratch_shapes=[pltpu.VMEM((seq_len, D_HEAD), BF16)] * 3 + [pltpu.VMEM((2, SB_BLOCK, D_HEAD), F32)],
        compiler_params=_cparams("parallel", "parallel"),
        name="sb_prompt",
    )(bias, proj, proj, proj)


def _sb_sample_kernel(n_new, n_step, pt_ref, bias_ref, q_ref, k_ref, v_ref, *refs):
    del pt_ref
    kp_refs, vp_refs = refs[:n_step], refs[n_step:2 * n_step]
    o_ref, q_bf, kn_ref, vn_ref, acc_ref, carry_ref = refs[2 * n_step + 1:]
    p = pl.program_id(1)
    page = kn_ref.shape[0]
    rows = SB_SAMPLE_ROWS
    upper = _upper(page)

    def heads(fn):
        for h in range(N_HEADS):
            fn(h, slice(h * D_HEAD, (h + 1) * D_HEAD))

    @pl.when(p == 0)
    def _():
        q_bf[...] = jnp.zeros_like(q_bf)
        q_bf[0:n_new, :] = q_ref[...] * (D_HEAD ** -0.5)
        kn_ref[...] = jnp.zeros_like(kn_ref)
        vn_ref[...] = jnp.zeros_like(vn_ref)
        kn_ref[0:n_new, :] = k_ref[...]
        vn_ref[0:n_new, :] = v_ref[...]
        mask = _iota2((rows, page), 1) < _iota2((rows, page), 0)

        def new_tokens(h, cols):
            out, carry = _sb_block(q_bf[:, cols].astype(BF16), kn_ref[:, cols].astype(BF16),
                                   vn_ref[:, cols].astype(BF16), bias_ref[h], upper, mask,
                                   jnp.zeros((rows, page), F32))
            acc_ref[:, cols] = out
            carry_ref[h * rows:(h + 1) * rows, :] = carry

        heads(new_tokens)

    def head_rows(ref, h):
        return ref[pl.ds(h, page, stride=N_HEADS), :].astype(BF16)

    z = jnp.concatenate(
        [_dot_nt(q_bf[:, h * D_HEAD:(h + 1) * D_HEAD].astype(BF16),
                 jnp.concatenate([head_rows(kp_ref, h) for kp_ref in kp_refs], axis=0)) + bias_ref[h]
         for h in range(N_HEADS)], axis=0)
    sp = _softplus_neg_abs(z)
    log_beta = jnp.minimum(z, 0.0) - sp
    log_keep = jnp.minimum(-z, 0.0) - sp
    hi, lo = _split_bf16(log_keep)
    stacked = jnp.concatenate([hi, lo], axis=0)
    n_rows = N_HEADS * rows
    carry = carry_ref[...]
    ws = []
    for j in range(n_step):
        seg = slice(j * page, (j + 1) * page)
        after2 = _dot(stacked[:, seg], upper)
        ws.append(jnp.exp(log_beta[:, seg] + (after2[:n_rows] + after2[n_rows:]) + carry).astype(BF16))
        carry = carry + jnp.sum(log_keep[:, seg], axis=1, keepdims=True)
    carry_ref[...] = carry
    w = jnp.concatenate(ws, axis=1)
    acc_ref[...] += jnp.concatenate(
        [_dot(w[h * rows:(h + 1) * rows], jnp.concatenate([head_rows(vp_ref, h) for vp_ref in vp_refs], axis=0))
         for h in range(N_HEADS)], axis=1)

    @pl.when(p == pl.num_programs(1) - 1)
    def _():
        o_ref[...] = acc_ref[0:n_new, :]


def _sb_sample(proj, bias, cache_k, cache_v, page_table, layer, n_seq, n_new, row_block, out_buf):
    t = proj.shape[0]
    n_pages = page_table.shape[1]
    page = cache_k.shape[2]
    assert n_new <= SB_SAMPLE_ROWS <= page
    n_step = SB_PAGES_PER_STEP if n_pages % SB_PAGES_PER_STEP == 0 else 1
    k_rows = cache_k.reshape(cache_k.shape[:2] + (page * N_HEADS, D_HEAD))
    v_rows = cache_v.reshape(cache_v.shape[:2] + (page * N_HEADS, D_HEAD))
    tok = lambda col: pl.BlockSpec((n_new, D_MODEL), lambda b, p, pt: (b + row_block, col // N_HEADS))

    def paged(j):
        return pl.BlockSpec((None, None, page * N_HEADS, D_HEAD),
                            lambda b, p, pt: (pt[b, n_pages - 1 - (p * n_step + j)], layer, 0, 0))

    pages = [paged(j) for j in range(n_step)]
    return pl.pallas_call(
        functools.partial(_sb_sample_kernel, n_new, n_step),
        out_shape=jax.ShapeDtypeStruct((t, D_MODEL), F32),
        grid_spec=pltpu.PrefetchScalarGridSpec(
            num_scalar_prefetch=1,
            grid=(n_seq, n_pages // n_step),
            in_specs=[pl.BlockSpec(memory_space=pltpu.SMEM), tok(COL_C_Q), tok(COL_C_K), tok(COL_C_V)]
            + pages + pages + [pl.BlockSpec(memory_space=pl.ANY)],
            out_specs=pl.BlockSpec((n_new, D_MODEL), lambda b, p, pt: (b + row_block, 0)),
            scratch_shapes=[pltpu.VMEM((SB_SAMPLE_ROWS, D_MODEL), F32),
                            pltpu.VMEM((page, D_MODEL), F32), pltpu.VMEM((page, D_MODEL), F32),
                            pltpu.VMEM((SB_SAMPLE_ROWS, D_MODEL), F32),
                            pltpu.VMEM((N_HEADS * SB_SAMPLE_ROWS, page), F32)],
        ),
        input_output_aliases={5 + 2 * n_step: 0},
        compiler_params=_cparams("parallel", "arbitrary"),
        name="sb_sample",
    )(page_table, bias, proj, proj, proj, *([k_rows] * n_step), *([v_rows] * n_step), out_buf)


def _token_tile(t):
    for tm in (1280, 1024, 512, 256, 128, 64, 32, 16, 8):
        if t % tm == 0:
            return tm
    raise ValueError(f"token count {t} is not a multiple of 8")


def _gate_rows(gates, n_seq, seq_len):
    g = gates[:, :2 * N_HEADS].reshape(n_seq, seq_len, 2, N_HEADS)
    return jnp.transpose(g, (0, 3, 2, 1))


def kernel(x_prompt, x_sample, state_hgrn, state_mlstm_C, state_mlstm_n, state_mlstm_m, cache_k, cache_v, page_table, meta_tokens, g_ffn1, w_ffn1_up, w_ffn1_down, g_mix, w_in, lb_param, b_ig, b_fg, b_sb, g_norm_a, g_norm_b, w_out, g_ffn2, w_ffn2_up, w_ffn2_down, g_final):
    depth = w_in.shape[0]
    n_p, s_p, _ = x_prompt.shape
    n_s, l_s, _ = x_sample.shape
    l_p = s_p + N_META
    t_p, t_s = n_p * l_p, n_s * l_s
    t = t_p + t_s
    tm = _token_tile(t)
    rb_s = t_p // l_s
    assert t_p % l_s == 0

    meta = jnp.broadcast_to(meta_tokens.astype(x_prompt.dtype)[None], (n_p, N_META, D_MODEL))
    h = jnp.concatenate([jnp.concatenate([meta, x_prompt], axis=1).reshape(t_p, D_MODEL),
                         x_sample.reshape(t_s, D_MODEL)], axis=0)

    lb = jnp.cumsum(jax.nn.softmax(lb_param.astype(F32), axis=0), axis=0)
    lb = lb - lb[0]
    lb_pack = jnp.stack([lb, jnp.log(lb), jnp.log1p(-lb)], axis=1)

    w_main = jnp.concatenate([w_in[:, :, :GATE_OFFSET], w_in[:, :, GATE_OFFSET + 2 * N_HEADS:]], axis=2).astype(BF16)
    w_gate = jnp.pad(w_in[:, :, GATE_OFFSET:GATE_OFFSET + 2 * N_HEADS],
                     ((0, 0), (0, 0), (0, LANES - 2 * N_HEADS))).astype(BF16)
    up1, down1 = w_ffn1_up.astype(BF16), w_ffn1_down.astype(BF16)
    up2, down2 = w_ffn2_up.astype(BF16), w_ffn2_down.astype(BF16)
    w_o = w_out.astype(BF16)
    gate_bias = jnp.concatenate([b_ig, b_fg], axis=1).astype(F32)
    m0 = jnp.broadcast_to(state_mlstm_m[..., None, None], state_mlstm_m.shape + (1, LANES))
    n0 = state_mlstm_n[:, :, :, None, :]

    outs = {k: [] for k in ("pS", "pC", "pn", "pm", "pk", "pv", "sS", "sC", "sn", "sm", "sk", "sv")}
    for l in range(depth):
        row = lambda a: a[l][None].astype(F32)
        h = _ffn(h, row(g_ffn1), up1[l], down1[l], tm)
        proj, gates = _inproj(h, row(g_mix), w_main[l], w_gate[l], tm)

        gn_a, gn_b = row(g_norm_a), row(g_norm_b)
        h_a, s_p_l = _hgrn(proj, lb_pack[l], gn_a, n_p, l_p, N_META, 0)
        h_a, s_s_l = _hgrn(proj, lb_pack[l], gn_a, n_s, l_s, 0, rb_s, state=state_hgrn[l], out_buf=h_a)

        gr_p = _gate_rows(gates[:t_p], n_p, l_p)
        gr_s = _gate_rows(gates[t_p:], n_s, l_s)
        h_b, c_p, nn_p, m_p = _mlstm(proj, gates, gr_p[..., :N_META], gr_p[..., N_META:], gate_bias[l], gn_b,
                                     n_p, l_p, N_META, 0)
        h_b, c_s, nn_s, m_s = _mlstm(proj, gates, None, gr_s, gate_bias[l], gn_b, n_s, l_s, 0, rb_s,
                                     state=(state_mlstm_C[l], n0[l], m0[l]), out_buf=h_b)

        sb_bias = b_sb[l].astype(F32)
        h_c = _sb_prompt(proj, sb_bias, n_p, l_p, N_META)
        h_c = _sb_sample(proj, sb_bias, cache_k, cache_v, page_table, l, n_s, l_s, rb_s, h_c)

        h = _merge(h, proj, h_a, h_b, h_c, w_o[l], tm)
        h = _ffn(h, row(g_ffn2), up2[l], down2[l], tm)

        k_c = proj[:, COL_C_K * LANES:(COL_C_K + N_HEADS) * LANES]
        v_c = proj[:, COL_C_V * LANES:(COL_C_V + N_HEADS) * LANES]
        outs["pS"].append(s_p_l)
        outs["pC"].append(c_p)
        outs["pn"].append(nn_p[:, :, 0])
        outs["pm"].append(m_p[:, :, 0, 0])
        outs["pk"].append(k_c[:t_p].reshape(n_p, l_p, N_HEADS, D_HEAD))
        outs["pv"].append(v_c[:t_p].reshape(n_p, l_p, N_HEADS, D_HEAD))
        outs["sS"].append(s_s_l)
        outs["sC"].append(c_s)
        outs["sn"].append(nn_s[:, :, 0])
        outs["sm"].append(m_s[:, :, 0, 0])
        outs["sk"].append(k_c[t_p:].reshape(n_s, l_s, N_HEADS, D_HEAD))
        outs["sv"].append(v_c[t_p:].reshape(n_s, l_s, N_HEADS, D_HEAD))

    g_fin = g_final[None].astype(F32)
    y_prompt = _final_norm(h, g_fin, n_p, l_p, N_META, 0).reshape(n_p, s_p, D_MODEL)
    y_sample = _final_norm(h, g_fin, n_s, l_s, 0, rb_s).reshape(n_s, l_s, D_MODEL)
    st = lambda k, axis=0: jnp.stack(outs[k], axis=axis)
    return (y_prompt, y_sample, st("pS"), st("pC"), st("pn"), st("pm"), st("pk", 1), st("pv", 1),
            st("sS"), st("sC"), st("sn"), st("sm"), st("sk", 1), st("sv", 1))
```
